```python
import math
import jax, jax.numpy as jnp
from jax import lax
import numpy as np

D_MODEL = 1024
BATCH = 1
SEQ = 16384
DEPTH = 4
DEC_BATCH = 32
DEC_SEQ = 2048
PAST_LEN = 128

D_CONV = 512
CONV_WIDTH = 31
D_RWKV = 512
RWKV_HEAD = 64
N_RWKV_HEADS = D_RWKV // RWKV_HEAD
DECAY_RANK = 64
ICLR_RANK = 64
GATE_RANK = 128
N_EXPERTS = 32
TOP_K = 4
D_EXPERT = 1024
SWIGLU_LIMIT = 7.0
SWIGLU_ALPHA = 1.702
MOE_BLOCK = 256
D_PLE = 256
LN_EPS = 1e-5
GN_EPS = 64e-5
DEEPNORM_ALPHA = (2 * DEPTH) ** 0.25
DEEPNORM_BETA = (8 * DEPTH) ** -0.25
N_GLU = 2 * D_CONV
N_SHIFT = 3 * D_RWKV + 2 * DECAY_RANK + 2 * ICLR_RANK + GATE_RANK
N_GATES = 2 * D_MODEL
N_IN = N_GLU + N_SHIFT + N_GATES

kernel_name = "hybrid_conv_rwkv7_moe_encoder"


def layer_norm(x, g, b, eps=LN_EPS):
    xf = x.astype(jnp.float32)
    mu = jnp.mean(xf, -1, keepdims=True)
    var = jnp.mean(jnp.square(xf - mu), -1, keepdims=True)
    return ((xf - mu) * lax.rsqrt(var + eps) * g + b).astype(x.dtype)


def split_sizes(z, sizes):
    out, off = [], 0
    for s in sizes:
        out.append(z[..., off:off + s])
        off += s
    return out


def centred_token_shift(z, mu):
    zp = jnp.pad(z, ((0, 0), (1, 0), (0, 0)))[:, :-1]
    zn = jnp.pad(z, ((0, 0), (0, 1), (0, 0)))[:, 1:]
    return z + mu[0] * (zp - z) + mu[1] * (zn - z)


def conv_branch(c, dw_w, dw_b, cln_g, cln_b, w_cb, b_cb):
    val, gate = jnp.split(c, 2, axis=-1)
    u = val * jax.nn.sigmoid(gate)
    u = lax.conv_general_dilated(
        u, dw_w[:, None, :].astype(u.dtype), window_strides=(1,),
        padding=[(CONV_WIDTH // 2, CONV_WIDTH // 2)],
        dimension_numbers=("NWC", "WIO", "NWC"),
        feature_group_count=D_CONV) + dw_b
    u = jax.nn.silu(layer_norm(u, cln_g, cln_b))
    return u @ w_cb + b_cb


def wkv7_scan(r, decay, k, v, kk, b, reverse):
    Bsz, T, H, N = r.shape

    def step(S, inp):
        r_t, w_t, k_t, v_t, kk_t, b_t = inp
        sa = jnp.einsum("bhvk,bhk->bhv", S, -kk_t)
        S_new = (S * w_t[:, :, None, :] + sa[..., None] * b_t[:, :, None, :]
                 + v_t[..., None] * k_t[:, :, None, :])
        S_read = S if reverse else S_new
        return S_new, jnp.einsum("bhvk,bhk->bhv", S_read, r_t)

    S0 = jnp.zeros((Bsz, H, N, N), jnp.float32)
    xs = tuple(jnp.moveaxis(z, 1, 0) for z in (r, decay, k, v, kk, b))
    _, y = lax.scan(step, S0, xs, reverse=reverse)
    return jnp.moveaxis(y, 0, 1)


def rwkv_branch(zs, w0, w2, a0, a2, g2, k_k, k_a, r_k, lnx_g, lnx_b, w_rb):
    Bsz, T, _ = zs.shape
    H, N = N_RWKV_HEADS, RWKV_HEAD
    r, k, v, lw_f, lw_b, la_f, la_b, lg = split_sizes(
        zs, [D_RWKV, D_RWKV, D_RWKV, DECAY_RANK, DECAY_RANK, ICLR_RANK, ICLR_RANK, GATE_RANK])

    def heads(z):
        return z.reshape(Bsz, T, H, N).astype(jnp.float32)

    r_h, k_h, v_h = heads(r), heads(k), heads(v)
    kk = k_h * k_k.reshape(H, N).astype(jnp.float32)
    kk = kk * lax.rsqrt(jnp.sum(kk * kk, -1, keepdims=True) + 1e-12)
    g = jax.nn.sigmoid(lg) @ g2
    k_a_h = k_a.reshape(H, N).astype(jnp.float32)

    y = jnp.zeros((Bsz, T, H, N), jnp.float32)
    for d, lw, la, rev in ((0, lw_f, la_f, False), (1, lw_b, la_b, True)):
        logw = -jax.nn.softplus(-(w0[d] + jnp.tanh(lw) @ w2[d]).astype(jnp.float32)) - 0.5
        decay = jnp.exp(-jnp.exp(logw))
        a = jax.nn.sigmoid((a0[d] + la @ a2[d]).astype(jnp.float32))
        a_h = heads(a)
        k_d = k_h * (1.0 + (a_h - 1.0) * k_a_h)
        y = y + wkv7_scan(r_h, heads(decay), k_d, v_h, kk, kk * a_h, rev)

    mu = jnp.mean(y, -1, keepdims=True)
    var = jnp.mean(jnp.square(y - mu), -1, keepdims=True)
    y = ((y - mu) * lax.rsqrt(var + GN_EPS)).reshape(Bsz, T, D_RWKV) * lnx_g + lnx_b
    bonus = jnp.sum(r_h * k_h * r_k.astype(jnp.float32), -1, keepdims=True) * v_h
    y = (y + bonus.reshape(Bsz, T, D_RWKV)).astype(zs.dtype) * g
    return y @ w_rb


def token_mixer(x, w_in, b_in, mu_shift, dw_w, dw_b, cln_g, cln_b, w_cb, b_cb,
                w0, w2, a0, a2, g2, k_k, k_a, r_k, lnx_g, lnx_b, w_rb, w_out, b_out):
    h = x @ w_in + b_in
    c, z, gt = split_sizes(h, [N_GLU, N_SHIFT, N_GATES])
    conv_out = conv_branch(c, dw_w, dw_b, cln_g, cln_b, w_cb, b_cb)
    zs = centred_token_shift(z, mu_shift)
    rwkv_out = rwkv_branch(zs, w0, w2, a0, a2, g2, k_k, k_a, r_k, lnx_g, lnx_b, w_rb)
    g_c, g_r = jnp.split(jax.nn.sigmoid(gt), 2, axis=-1)
    return (g_c * conv_out + g_r * rwkv_out) @ w_out + b_out


def moe(x, w_router, b_router, w_gu, b_gu, w_down, b_down):
    Bsz, T, D = x.shape
    M = Bsz * T
    xt = x.reshape(M, D)
    logits = (xt @ w_router + b_router).astype(jnp.float32)
    top_logit, top_e = lax.top_k(logits, TOP_K)
    gate = jax.nn.softmax(top_logit, axis=-1)
    n_assign = M * TOP_K
    flat_e = top_e.reshape(-1)
    order = jnp.argsort(flat_e)
    sorted_e = flat_e[order]
    counts = jnp.bincount(flat_e, length=N_EXPERTS)
    padded = (counts + MOE_BLOCK - 1) // MOE_BLOCK * MOE_BLOCK
    start = jnp.cumsum(counts) - counts
    pend = jnp.cumsum(padded)
    pstart = pend - padded
    dest = pstart[sorted_e] + jnp.arange(n_assign, dtype=jnp.int32) - start[sorted_e]
    n_blocks = -(-n_assign // MOE_BLOCK) + N_EXPERTS
    n_rows = n_blocks * MOE_BLOCK
    row_tok = jnp.full((n_rows,), M, jnp.int32).at[dest].set((order // TOP_K).astype(jnp.int32))
    row_w = jnp.zeros((n_rows,), jnp.float32).at[dest].set(gate.reshape(-1)[order])
    blk_e = jnp.minimum(
        jnp.searchsorted(pend, jnp.arange(n_blocks, dtype=jnp.int32) * MOE_BLOCK, side="right"),
        N_EXPERTS - 1)
    x_pad = jnp.concatenate([xt, jnp.zeros((1, D), xt.dtype)], axis=0)

    def expert_block(args):
        tok, wt, e = args
        hgu = x_pad[tok] @ w_gu[e] + b_gu[e]
        hg, hl = jnp.split(hgu, 2, axis=-1)
        hg = jnp.minimum(hg, SWIGLU_LIMIT)
        hl = jnp.clip(hl, -SWIGLU_LIMIT, SWIGLU_LIMIT)
        act = hg * jax.nn.sigmoid(SWIGLU_ALPHA * hg) * (hl + 1.0)
        yb = act @ w_down[e] + b_down[e]
        return yb * wt[:, None].astype(yb.dtype)

    y_rows = lax.map(expert_block, (row_tok.reshape(n_blocks, MOE_BLOCK),
                                    row_w.reshape(n_blocks, MOE_BLOCK), blk_e))
    y = jax.ops.segment_sum(y_rows.reshape(n_rows, D), row_tok, num_segments=M + 1)[:M]
    return y.reshape(Bsz, T, D)


def encoder_trunk(x, p, prm):
    x = layer_norm(x, prm["ln0_g"], prm["ln0_b"])
    for i in range(DEPTH):
        mix = token_mixer(
            x, prm["w_in"][i], prm["b_in"][i], prm["mu_shift"][i],
            prm["dw_w"][i], prm["dw_b"][i], prm["cln_g"][i], prm["cln_b"][i],
            prm["w_cb"][i], prm["b_cb"][i],
            prm["w0"][i], prm["w2"][i], prm["a0"][i], prm["a2"][i], prm["g2"][i],
            prm["k_k"][i], prm["k_a"][i], prm["r_k"][i], prm["lnx_g"][i], prm["lnx_b"][i],
            prm["w_rb"][i], prm["w_out"][i], prm["b_out"][i])
        x = layer_norm(DEEPNORM_ALPHA * x + mix, prm["ln1_g"][i], prm["ln1_b"][i])
        ffn = moe(x, prm["w_router"][i], prm["b_router"][i], prm["w_gu"][i], prm["b_gu"][i],
                  prm["w_down"][i], prm["b_down"][i])
        ple = jax.nn.sigmoid(x @ prm["w_pg"][i] + prm["b_pg"][i]) * (p[i] @ prm["w_pp"][i])
        x = layer_norm(DEEPNORM_ALPHA * x + ffn + ple, prm["ln2_g"][i], prm["ln2_b"][i])
    return x


def setup_inputs(seed: int = 0) -> dict:
    key = jax.random.key(seed)
    ks = iter(jax.random.split(key, 64))
    f32 = jnp.float32

    def nrm(shape, scale):
        return jax.random.normal(next(ks), shape, f32) * scale

    def unif(shape, lo, hi):
        return jax.random.uniform(next(ks), shape, f32, lo, hi)

    L, D = DEPTH, D_MODEL
    return {
        "x_prompt": nrm((BATCH, SEQ, D), 1.0),
        "x_sample": nrm((DEC_BATCH, DEC_SEQ, D), 1.0),
        "p_prompt": nrm((DEPTH, BATCH, SEQ, D_PLE), 1.0),
        "p_sample": nrm((DEPTH, DEC_BATCH, DEC_SEQ, D_PLE), 1.0),
        "ln0_g": 1.0 + nrm((D,), 0.01),
        "ln0_b": nrm((D,), 0.01),
        "w_in": nrm((L, D, N_IN), D ** -0.5),
        "b_in": nrm((L, N_IN), 0.01),
        "mu_shift": unif((L, 2, N_SHIFT), 0.0, 0.5),
        "dw_w": nrm((L, CONV_WIDTH, D_CONV), CONV_WIDTH ** -0.5),
        "dw_b": nrm((L, D_CONV), 0.01),
        "cln_g": 1.0 + nrm((L, D_CONV), 0.01),
        "cln_b": nrm((L, D_CONV), 0.01),
        "w_cb": nrm((L, D_CONV, D), D_CONV ** -0.5),
        "b_cb": nrm((L, D), 0.01),
        "w0": unif((L, 2, D_RWKV), -6.0, 0.0),
        "w2": nrm((L, 2, DECAY_RANK, D_RWKV), 0.5 * DECAY_RANK ** -0.5),
        "a0": nrm((L, 2, D_RWKV), 0.5),
        "a2": nrm((L, 2, ICLR_RANK, D_RWKV), ICLR_RANK ** -0.5),
        "g2": nrm((L, GATE_RANK, D_RWKV), GATE_RANK ** -0.5),
        "k_k": 0.85 + nrm((L, D_RWKV), 0.05),
        "k_a": 1.0 + nrm((L, D_RWKV), 0.05),
        "r_k": nrm((L, N_RWKV_HEADS, RWKV_HEAD), 0.1),
        "lnx_g": 1.0 + nrm((L, D_RWKV), 0.01),
        "lnx_b": nrm((L, D_RWKV), 0.01),
        "w_rb": nrm((L, D_RWKV, D), D_RWKV ** -0.5),
        "w_out": nrm((L, D, D), DEEPNORM_BETA * D ** -0.5),
        "b_out": nrm((L, D), 0.01),
        "ln1_g": 1.0 + nrm((L, D), 0.01),
        "ln1_b": nrm((L, D), 0.01),
        "w_router": nrm((L, D, N_EXPERTS), D ** -0.5),
        "b_router": nrm((L, N_EXPERTS), 0.01),
        "w_gu": nrm((L, N_EXPERTS, D, 2 * D_EXPERT), D ** -0.5),
        "b_gu": nrm((L, N_EXPERTS, 2 * D_EXPERT), 0.01),
        "w_down": nrm((L, N_EXPERTS, D_EXPERT, D), DEEPNORM_BETA * D_EXPERT ** -0.5),
        "b_down": nrm((L, N_EXPERTS, D), 0.01),
        "w_pg": nrm((L, D, D), D ** -0.5),
        "b_pg": nrm((L, D), 0.01),
        "w_pp": nrm((L, D_PLE, D), DEEPNORM_BETA * D_PLE ** -0.5),
        "ln2_g": 1.0 + nrm((L, D), 0.01),
        "ln2_b": nrm((L, D), 0.01),
    }


def reference(x_prompt, x_sample, p_prompt, p_sample, ln0_g, ln0_b, w_in, b_in, mu_shift,
              dw_w, dw_b, cln_g, cln_b, w_cb, b_cb, w0, w2, a0, a2, g2, k_k, k_a, r_k,
              lnx_g, lnx_b, w_rb, w_out, b_out, ln1_g, ln1_b, w_router, b_router,
              w_gu, b_gu, w_down, b_down, w_pg, b_pg, w_pp, ln2_g, ln2_b):
    prm = dict(ln0_g=ln0_g, ln0_b=ln0_b, w_in=w_in, b_in=b_in, mu_shift=mu_shift,
               dw_w=dw_w, dw_b=dw_b, cln_g=cln_g, cln_b=cln_b, w_cb=w_cb, b_cb=b_cb,
               w0=w0, w2=w2, a0=a0, a2=a2, g2=g2, k_k=k_k, k_a=k_a, r_k=r_k,
               lnx_g=lnx_g, lnx_b=lnx_b, w_rb=w_rb, w_out=w_out, b_out=b_out,
               ln1_g=ln1_g, ln1_b=ln1_b, w_router=w_router, b_router=b_router,
               w_gu=w_gu, b_gu=b_gu, w_down=w_down, b_down=b_down,
               w_pg=w_pg, b_pg=b_pg, w_pp=w_pp, ln2_g=ln2_g, ln2_b=ln2_b)
    y_prompt = encoder_trunk(x_prompt, p_prompt, prm)
    y_sample = encoder_trunk(x_sample, p_sample, prm)
    return (y_prompt, y_sample)
```

```python
import functools
import math

import jax
import jax.numpy as jnp
from jax import lax
from jax.experimental import pallas as pl
from jax.experimental.pallas import tpu as pltpu

F32 = jnp.float32
BF16 = jnp.bfloat16

D_MODEL = 1024
DEPTH = 4
D_CONV = 512
CONV_WIDTH = 31
CONV_HALO = 16
D_RWKV = 512
RWKV_HEAD = 64
DECAY_RANK = 64
ICLR_RANK = 64
GATE_RANK = 128
N_EXPERTS = 32
TOP_K = 4
D_EXPERT = 1024
SWIGLU_LIMIT = 7.0
SWIGLU_ALPHA = 1.702
D_PLE = 256
LN_EPS = 1e-5
GN_EPS = 64e-5
DEEPNORM_ALPHA = (2 * DEPTH) ** 0.25
DECAY_SCALE = math.exp(-0.5)
N_GLU = 2 * D_CONV
N_SHIFT = 3 * D_RWKV + 2 * DECAY_RANK + 2 * ICLR_RANK + GATE_RANK
N_GATES = 2 * D_MODEL
N_LOWRANK = 2 * DECAY_RANK

LANES = 128
SHIFT_HALO = 8
TOK_TILE = 256
MOE_TILE = 512
WKV_CHUNK = 64
WKV_GROUP = 4
WKV_GW = WKV_GROUP * RWKV_HEAD
VMEM_LIMIT = 48 * 1024 * 1024
assert WKV_CHUNK == RWKV_HEAD


def _cparams(n_axes=1):
    return pltpu.CompilerParams(dimension_semantics=("arbitrary",) * n_axes,
                                vmem_limit_bytes=VMEM_LIMIT)


def _dot(a, b):
    return jnp.dot(a.astype(BF16), b.astype(BF16), preferred_element_type=F32)


def _dot_nt(a, b):
    return lax.dot_general(a.astype(BF16), b.astype(BF16), (((1,), (1,)), ((), ())),
                           preferred_element_type=F32)


def _dot_tn(a, b):
    return lax.dot_general(a.astype(BF16), b.astype(BF16), (((0,), (0,)), ((), ())),
                           preferred_element_type=F32)


def _split_bf16(x):
    hi = x.astype(BF16)
    lo = (x - hi.astype(F32)).astype(BF16)
    return hi, lo


def _dot_hilo(x, w01):
    hi, lo = _split_bf16(x)
    return (jnp.dot(hi, w01, preferred_element_type=F32)
            + jnp.dot(lo, w01, preferred_element_type=F32))


def _layer_norm(x, g, b, eps):
    mu = jnp.mean(x, axis=-1, keepdims=True)
    xc = x - mu
    var = jnp.mean(xc * xc, axis=-1, keepdims=True)
    return xc * lax.rsqrt(var + eps) * g + b


def _seq_start(tok, m_prompt, t_prompt, t_sample, m_total):
    in_prompt = tok < m_prompt
    at_prompt_start = lax.rem(tok, jnp.int32(t_prompt)) == 0
    at_sample_start = lax.rem(jnp.abs(tok - m_prompt), jnp.int32(t_sample)) == 0
    return ((in_prompt & at_prompt_start) | (jnp.logical_not(in_prompt) & at_sample_start)
            | (tok >= m_total))


def _const_spec(shape, layer=None):
    if layer is None:
        nd = len(shape)
        return pl.BlockSpec(shape, lambda i, _n=nd: (0,) * _n)
    nd = len(shape)
    return pl.BlockSpec((None,) + tuple(shape), lambda i, _l=layer, _n=nd: (_l,) + (0,) * _n)


def _row_spec(tm, width):
    return pl.BlockSpec((tm, width), lambda i: (i, 0))


def _ln_kernel(x_ref, g_ref, b_ref, o_ref):
    o_ref[...] = _layer_norm(x_ref[...], g_ref[...], b_ref[...], LN_EPS)


def _ln0(x, g, b):
    m = x.shape[0]
    tm = 512
    return pl.pallas_call(
        _ln_kernel, grid=(m // tm,),
        in_specs=[_row_spec(tm, D_MODEL), _const_spec((1, D_MODEL)), _const_spec((1, D_MODEL))],
        out_specs=_row_spec(tm, D_MODEL),
        out_shape=jax.ShapeDtypeStruct((m, D_MODEL), F32),
        compiler_params=_cparams(), name="ln0")(x, g.reshape(1, -1), b.reshape(1, -1))


def _in_proj_kernel(x_ref, wc_ref, wz_ref, wg_ref, bc_ref, bz_ref, bg_ref, c_ref, z_ref, gt_ref):
    xb = x_ref[...].astype(BF16)
    c_ref[...] = jnp.dot(xb, wc_ref[...], preferred_element_type=F32) + bc_ref[...]
    z_ref[...] = jnp.dot(xb, wz_ref[...], preferred_element_type=F32) + bz_ref[...]
    gt_ref[...] = jnp.dot(xb, wg_ref[...], preferred_element_type=F32) + bg_ref[...]


def _in_proj(x, wc, wz, wg, bc, bz, bg, layer):
    m = x.shape[0]
    tm = TOK_TILE
    return pl.pallas_call(
        _in_proj_kernel, grid=(m // tm,),
        in_specs=[_row_spec(tm, D_MODEL),
                  _const_spec((D_MODEL, N_GLU), layer), _const_spec((D_MODEL, N_SHIFT), layer),
                  _const_spec((D_MODEL, N_GATES), layer),
                  _const_spec((1, N_GLU), layer), _const_spec((1, N_SHIFT), layer),
                  _const_spec((1, N_GATES), layer)],
        out_specs=[_row_spec(tm, N_GLU), _row_spec(tm, N_SHIFT), _row_spec(tm, N_GATES)],
        out_shape=[jax.ShapeDtypeStruct((m, N_GLU), F32), jax.ShapeDtypeStruct((m, N_SHIFT), F32),
                   jax.ShapeDtypeStruct((m, N_GATES), F32)],
        compiler_params=_cparams(), name="in_proj")(x, wc, wz, wg, bc, bz, bg)


def _conv_kernel(seq, c_ref, cp_ref, cn_ref, dww_ref, dwb_ref, g_ref, b_ref, wcb_ref, bcb_ref,
                 o_ref, u_ref):
    tm = c_ref.shape[0]
    i = pl.program_id(0)
    first = _seq_start(i * tm, *seq)
    last = _seq_start((i + 1) * tm, *seq)

    def glu(c):
        return c[:, :D_CONV] * jax.nn.sigmoid(c[:, D_CONV:])

    u_ref[0:CONV_HALO, :] = jnp.where(first, 0.0, glu(cp_ref[...]))
    u_ref[CONV_HALO:CONV_HALO + tm, :] = glu(c_ref[...])
    u_ref[CONV_HALO + tm:, :] = jnp.where(last, 0.0, glu(cn_ref[...]))

    base = CONV_HALO - CONV_WIDTH // 2
    acc = jnp.zeros((tm, D_CONV), F32) + dwb_ref[...]
    for j in range(CONV_WIDTH):
        acc = acc + u_ref[base + j:base + j + tm, :] * dww_ref[j:j + 1, :]
    u = _layer_norm(acc, g_ref[...], b_ref[...], LN_EPS)
    u = u * jax.nn.sigmoid(u)
    o_ref[...] = _dot(u, wcb_ref[...]) + bcb_ref[...]


def _conv_branch(c, dww, dwb, g, b, wcb, bcb, layer, seq):
    m = c.shape[0]
    tm = TOK_TILE
    r = tm // CONV_HALO
    nh = m // CONV_HALO
    return pl.pallas_call(
        functools.partial(_conv_kernel, seq), grid=(m // tm,),
        in_specs=[_row_spec(tm, N_GLU),
                  pl.BlockSpec((CONV_HALO, N_GLU), lambda i: (jnp.maximum(i * r - 1, 0), 0)),
                  pl.BlockSpec((CONV_HALO, N_GLU), lambda i: (jnp.minimum((i + 1) * r, nh - 1), 0)),
                  _const_spec((CONV_WIDTH, D_CONV), layer), _const_spec((1, D_CONV), layer),
                  _const_spec((1, D_CONV), layer), _const_spec((1, D_CONV), layer),
                  _const_spec((D_CONV, D_MODEL), layer), _const_spec((1, D_MODEL), layer)],
        out_specs=_row_spec(tm, D_MODEL),
        out_shape=jax.ShapeDtypeStruct((m, D_MODEL), F32),
        scratch_shapes=[pltpu.VMEM((tm + 2 * CONV_HALO, D_CONV), F32)],
        compiler_params=_cparams(), name="conv_branch")(c, c, c, dww, dwb, g, b, wcb, bcb)


def _rwkv_prep_kernel(seq, z_ref, zp_ref, zn_ref, mu_ref, w0_ref, w2_ref, a0_ref, a2_ref, g2_ref,
                      kk_ref, ka_ref, rk_ref, hs_ref,
                      r_ref, v_ref, kkn_ref, g_ref, bonus_ref,
                      lwf_ref, kdf_ref, bf_ref, lwb_ref, kdb_ref, bb_ref):
    tm = z_ref.shape[0]
    i = pl.program_id(0)
    first = _seq_start(i * tm, *seq)
    last = _seq_start((i + 1) * tm, *seq)

    z = z_ref[...]
    row = lax.broadcasted_iota(jnp.int32, (tm, 1), 0)
    prev_row = jnp.where(first, 0.0, zp_ref[SHIFT_HALO - 1:SHIFT_HALO, :])
    next_row = jnp.where(last, 0.0, zn_ref[0:1, :])
    zp = jnp.where(row == 0, prev_row, pltpu.roll(z, 1, 0))
    zn = jnp.where(row == tm - 1, next_row, pltpu.roll(z, tm - 1, 0))
    zs = z + mu_ref[0:1, :] * (zp - z) + mu_ref[1:2, :] * (zn - z)

    o = 3 * D_RWKV
    r = zs[:, 0:D_RWKV]
    k = zs[:, D_RWKV:2 * D_RWKV]
    v = zs[:, 2 * D_RWKV:o]
    lw = zs[:, o:o + N_LOWRANK]
    la = zs[:, o + N_LOWRANK:o + 2 * N_LOWRANK]
    lg = zs[:, o + 2 * N_LOWRANK:]

    hs = hs_ref[...]
    kk = k * kk_ref[...]
    kk = kk * lax.rsqrt(_dot_hilo(kk * kk, hs) + 1e-12)
    r_ref[...] = r
    v_ref[...] = v
    kkn_ref[...] = kk
    g_ref[...] = _dot(jax.nn.sigmoid(lg), g2_ref[...])
    bonus_ref[...] = _dot_hilo(r * k * rk_ref[...], hs) * v

    tlw = jnp.tanh(lw)
    ka = ka_ref[...]
    for d, (lw_o, kd_o, b_o) in enumerate(((lwf_ref, kdf_ref, bf_ref), (lwb_ref, kdb_ref, bb_ref))):
        t = w0_ref[d:d + 1, :] + _dot(tlw, w2_ref[d])
        a = jax.nn.sigmoid(a0_ref[d:d + 1, :] + _dot(la, a2_ref[d]))
        lw_o[...] = -DECAY_SCALE * jax.nn.sigmoid(t)
        kd_o[...] = k * (1.0 + (a - 1.0) * ka)
        b_o[...] = kk * a


def _rwkv_prep(z, mu, w0, w2p, a0, a2p, g2, k_k, k_a, r_k, hs, layer, seq):
    m = z.shape[0]
    tm = TOK_TILE
    r = tm // SHIFT_HALO
    nh = m // SHIFT_HALO
    outs = [jax.ShapeDtypeStruct((m, D_RWKV), F32)] * 11
    return pl.pallas_call(
        functools.partial(_rwkv_prep_kernel, seq), grid=(m // tm,),
        in_specs=[_row_spec(tm, N_SHIFT),
                  pl.BlockSpec((SHIFT_HALO, N_SHIFT), lambda i: (jnp.maximum(i * r - 1, 0), 0)),
                  pl.BlockSpec((SHIFT_HALO, N_SHIFT), lambda i: (jnp.minimum((i + 1) * r, nh - 1), 0)),
                  _const_spec((2, N_SHIFT), layer), _const_spec((2, D_RWKV), layer),
                  _const_spec((2, N_LOWRANK, D_RWKV), layer), _const_spec((2, D_RWKV), layer),
                  _const_spec((2, N_LOWRANK, D_RWKV), layer), _const_spec((GATE_RANK, D_RWKV), layer),
                  _const_spec((1, D_RWKV), layer), _const_spec((1, D_RWKV), layer),
                  _const_spec((1, D_RWKV), layer), _const_spec((D_RWKV, D_RWKV))],
        out_specs=[_row_spec(tm, D_RWKV)] * 11,
        out_shape=outs,
        compiler_params=_cparams(), name="rwkv_prep")(z, z, z, mu, w0, w2p, a0, a2p, g2, k_k, k_a, r_k, hs)


def _tile_rows(x, n):
    return jnp.concatenate([x] * n, axis=0)


def _fold_rows(x, n):
    h = x.shape[0] // n
    acc = x[0:h]
    for j in range(1, n):
        acc = acc + x[j * h:(j + 1) * h]
    return acc


def _wkv_kernel(reverse, seq, r_ref, v_ref, kk_ref, lw_ref, kd_ref, b_ref, y_ref, s_ref):
    C, G, GW = WKV_CHUNK, WKV_GROUP, WKV_GW
    n_chunks = pl.num_programs(0)
    i = pl.program_id(0)
    chunk = (n_chunks - 1 - i) if reverse else i
    if reverse:
        end = (chunk + 1) * C
        fresh = _seq_start(end, *seq)
    else:
        fresh = _seq_start(chunk * C, *seq)

    @pl.when(fresh)
    def _():
        s_ref[...] = jnp.zeros_like(s_ref)

    t_c = lax.broadcasted_iota(jnp.int32, (C, C), 0)
    s_c = lax.broadcasted_iota(jnp.int32, (C, C), 1)
    cum_mat = ((s_c >= t_c) if reverse else (s_c <= t_c)).astype(BF16)
    rows = lax.broadcasted_iota(jnp.int32, (GW, GW), 0)
    cols = lax.broadcasted_iota(jnp.int32, (GW, GW), 1)
    head_shift = RWKV_HEAD.bit_length() - 1
    same_head = (rows >> head_shift) == (cols >> head_shift)
    t_i, s_i = rows & (C - 1), cols & (C - 1)
    earlier = same_head & ((s_i > t_i) if reverse else (s_i < t_i))
    readable = earlier if reverse else (same_head & (s_i <= t_i))
    own_lanes = same_head
    eye = (rows == cols).astype(F32)
    last_row = 0 if reverse else C - 1

    for q in range(D_RWKV // GW):
        sl = slice(q * GW, (q + 1) * GW)
        lw = lw_ref[:, sl]
        lw_hi, lw_lo = _split_bf16(lw)
        cum = (jnp.dot(cum_mat, lw_hi, preferred_element_type=F32)
               + jnp.dot(cum_mat, lw_lo, preferred_element_type=F32))
        g_in = jnp.exp(cum)
        g_ex = jnp.exp(cum - lw)
        g_inv = jnp.exp(-cum)
        g_end = jnp.exp(cum[last_row:last_row + 1, :])
        al = -kk_ref[:, sl] * g_ex
        be = b_ref[:, sl] * g_inv
        kb = kd_ref[:, sl] * g_inv
        rb = r_ref[:, sl] * (g_ex if reverse else g_in)
        v = v_ref[:, sl]
        state = s_ref[q]

        def bd(x):
            return jnp.where(own_lanes, _tile_rows(x, G), 0.0)

        lhs = jnp.concatenate([bd(al), bd(rb)], axis=0)
        rhs = jnp.concatenate([_tile_rows(be, G), _tile_rows(kb, G)], axis=0)
        aa = _dot_nt(lhs, rhs)
        a_ab = jnp.where(earlier, aa[:GW, :GW], 0.0)
        a_ak = jnp.where(earlier, aa[:GW, GW:], 0.0)
        a_rb = jnp.where(readable, aa[GW:, :GW], 0.0)
        a_rk = jnp.where(readable, aa[GW:, GW:], 0.0)

        inv = eye + a_ab
        p = a_ab
        for _ in range(C.bit_length() - 2):
            p = _dot(p, p)
            inv = inv + _dot(p, inv)

        from_state = _dot_nt(jnp.concatenate([al, rb], axis=0), state)
        v_bd = bd(v)
        av = _dot(jnp.concatenate([a_ak, a_rk], axis=0), v_bd)
        u_bd = _dot(inv, bd(from_state[:C]) + av[:GW])
        y_bd = _dot(a_rb, u_bd) + av[GW:]
        y_ref[:, sl] = from_state[C:] + _fold_rows(y_bd, G)

        u = _fold_rows(u_bd, G)
        upd = _dot_tn(jnp.concatenate([u, v], axis=0),
                      jnp.concatenate([be * g_end, kb * g_end], axis=0))
        s_ref[q] = state * g_end + jnp.where(same_head, upd, 0.0)


def _wkv(r, v, kk, lw, kd, b, reverse, seq):
    m = r.shape[0]
    n_chunks = m // WKV_CHUNK
    if reverse:
        imap = lambda i: (n_chunks - 1 - i, 0)
    else:
        imap = lambda i: (i, 0)
    spec = pl.BlockSpec((WKV_CHUNK, D_RWKV), imap)
    return pl.pallas_call(
        functools.partial(_wkv_kernel, reverse, seq), grid=(n_chunks,),
        in_specs=[spec] * 6, out_specs=spec,
        out_shape=jax.ShapeDtypeStruct((m, D_RWKV), F32),
        scratch_shapes=[pltpu.VMEM((D_RWKV // WKV_GW, WKV_GW, WKV_GW), F32)],
        compiler_params=_cparams(), name="wkv_bwd" if reverse else "wkv_fwd")(r, v, kk, lw, kd, b)


def _post_kernel(x_ref, yf_ref, yb_ref, bonus_ref, g_ref, conv_ref, gt_ref, hs_ref,
                 lnxg_ref, lnxb_ref, wrb_ref, wout_ref, bout_ref, ln1g_ref, ln1b_ref,
                 wrh_ref, wrl_ref, br_ref,
                 x1_ref, x1b_ref, tope_ref, gate_ref):
    hs = hs_ref[...]
    y = yf_ref[...] + yb_ref[...]
    inv_n = 1.0 / RWKV_HEAD
    mu = _dot_hilo(y, hs) * inv_n
    yc = y - mu
    var = _dot_hilo(yc * yc, hs) * inv_n
    y = yc * lax.rsqrt(var + GN_EPS) * lnxg_ref[...] + lnxb_ref[...]
    y = (y + bonus_ref[...]) * g_ref[...]
    rwkv_out = _dot(y, wrb_ref[...])
    gt = gt_ref[...]
    mixed = (jax.nn.sigmoid(gt[:, :D_MODEL]) * conv_ref[...]
             + jax.nn.sigmoid(gt[:, D_MODEL:]) * rwkv_out)
    mix = _dot(mixed, wout_ref[...]) + bout_ref[...]
    x1 = _layer_norm(DEEPNORM_ALPHA * x_ref[...] + mix, ln1g_ref[...], ln1b_ref[...], LN_EPS)
    x1_ref[...] = x1
    x1b_ref[...] = x1.astype(BF16)

    xh, xl = _split_bf16(x1)
    logits = (jnp.dot(xh, wrh_ref[...], preferred_element_type=F32)
              + jnp.dot(xh, wrl_ref[...], preferred_element_type=F32)
              + jnp.dot(xl, wrh_ref[...], preferred_element_type=F32)) + br_ref[...]
    lane = lax.broadcasted_iota(jnp.int32, logits.shape, 1).astype(F32)
    neg = jnp.float32(-jnp.inf)
    logits = jnp.where(lane < N_EXPERTS, logits, neg)
    tops, idxs = [], []
    for _ in range(TOP_K):
        mx = jnp.max(logits, axis=-1, keepdims=True)
        ix = jnp.min(jnp.where(logits == mx, lane, float(LANES)), axis=-1, keepdims=True)
        tops.append(mx)
        idxs.append(ix)
        logits = jnp.where(lane == ix, neg, logits)
    es = [jnp.exp(t - tops[0]) for t in tops]
    denom = es[0] + es[1] + es[2] + es[3]
    tope = jnp.zeros(logits.shape, F32)
    gate = jnp.zeros(logits.shape, F32)
    for j in range(TOP_K):
        tope = jnp.where(lane == j, idxs[j], tope)
        gate = jnp.where(lane == j, es[j] / denom, gate)
    tope_ref[...] = tope.astype(jnp.int32)
    gate_ref[...] = gate


def _post(x, yf, yb, bonus, g, conv, gt, hs, lnxg, lnxb, wrb, wout, bout, ln1g, ln1b,
          wrh, wrl, br, layer):
    m = x.shape[0]
    tm = TOK_TILE
    return pl.pallas_call(
        _post_kernel, grid=(m // tm,),
        in_specs=[_row_spec(tm, D_MODEL)] + [_row_spec(tm, D_RWKV)] * 4
                 + [_row_spec(tm, D_MODEL), _row_spec(tm, N_GATES), _const_spec((D_RWKV, D_RWKV)),
                    _const_spec((1, D_RWKV), layer), _const_spec((1, D_RWKV), layer),
                    _const_spec((D_RWKV, D_MODEL), layer), _const_spec((D_MODEL, D_MODEL), layer),
                    _const_spec((1, D_MODEL), layer), _const_spec((1, D_MODEL), layer),
                    _const_spec((1, D_MODEL), layer),
                    _const_spec((D_MODEL, LANES), layer), _const_spec((D_MODEL, LANES), layer),
                    _const_spec((1, LANES), layer)],
        out_specs=[_row_spec(tm, D_MODEL), _row_spec(tm, D_MODEL), _row_spec(tm, LANES),
                   _row_spec(tm, LANES)],
        out_shape=[jax.ShapeDtypeStruct((m, D_MODEL), F32), jax.ShapeDtypeStruct((m, D_MODEL), BF16),
                   jax.ShapeDtypeStruct((m, LANES), jnp.int32), jax.ShapeDtypeStruct((m, LANES), F32)],
        compiler_params=_cparams(), name="post")(
            x, yf, yb, bonus, g, conv, gt, hs, lnxg, lnxb, wrb, wout, bout, ln1g, ln1b, wrh, wrl, br)


def _moe_kernel(blk_e_ref, nvalid_ref, x_ref, wgu_ref, bgu_ref, wd_ref, bd_ref, o_ref):
    i = pl.program_id(0)

    @pl.when(i < nvalid_ref[0])
    def _():
        hgu = jnp.dot(x_ref[...], wgu_ref[...], preferred_element_type=F32) + bgu_ref[...]
        hg = jnp.minimum(hgu[:, :D_EXPERT], SWIGLU_LIMIT)
        hl = jnp.clip(hgu[:, D_EXPERT:], -SWIGLU_LIMIT, SWIGLU_LIMIT)
        act = hg * jax.nn.sigmoid(SWIGLU_ALPHA * hg) * (hl + 1.0)
        o_ref[...] = _dot(act, wd_ref[...]) + bd_ref[...]

    @pl.when(i >= nvalid_ref[0])
    def _():
        o_ref[...] = jnp.zeros_like(o_ref)


def _moe(blk_e, nvalid, xs, wgu, bgu, wd, bd, layer):
    n_rows = xs.shape[0]
    tm = MOE_TILE
    grid_spec = pltpu.PrefetchScalarGridSpec(
        num_scalar_prefetch=2, grid=(n_rows // tm,),
        in_specs=[pl.BlockSpec((tm, D_MODEL), lambda i, be, nv: (i, 0)),
                  pl.BlockSpec((None, None, D_MODEL, 2 * D_EXPERT), lambda i, be, nv: (layer, be[i], 0, 0)),
                  pl.BlockSpec((None, None, 1, 2 * D_EXPERT), lambda i, be, nv: (layer, be[i], 0, 0)),
                  pl.BlockSpec((None, None, D_EXPERT, D_MODEL), lambda i, be, nv: (layer, be[i], 0, 0)),
                  pl.BlockSpec((None, None, 1, D_MODEL), lambda i, be, nv: (layer, be[i], 0, 0))],
        out_specs=pl.BlockSpec((tm, D_MODEL), lambda i, be, nv: (i, 0)))
    return pl.pallas_call(
        _moe_kernel, grid_spec=grid_spec,
        out_shape=jax.ShapeDtypeStruct((n_rows, D_MODEL), F32),
        compiler_params=_cparams(), name="moe")(blk_e, nvalid, xs, wgu, bgu, wd, bd)


def _route(top_e, tile):
    m = top_e.shape[0]
    n_assign = m * TOP_K
    flat_e = top_e.reshape(-1)
    order = jnp.argsort(flat_e, stable=True).astype(jnp.int32)
    sorted_e = flat_e[order]
    start = jnp.searchsorted(sorted_e, jnp.arange(N_EXPERTS, dtype=jnp.int32), side="left").astype(jnp.int32)
    counts = jnp.diff(jnp.concatenate([start, jnp.array([n_assign], jnp.int32)]))
    padded = (counts + tile - 1) // tile * tile
    pend = jnp.cumsum(padded).astype(jnp.int32)
    pstart = pend - padded
    dest = pstart[sorted_e] + jnp.arange(n_assign, dtype=jnp.int32) - start[sorted_e]
    n_blocks = n_assign // tile + N_EXPERTS
    row_tok = jnp.zeros((n_blocks * tile,), jnp.int32).at[dest].set(order // TOP_K)
    pos = jnp.zeros((n_assign,), jnp.int32).at[order].set(dest)
    blk_e = jnp.minimum(
        jnp.searchsorted(pend, jnp.arange(n_blocks, dtype=jnp.int32) * tile, side="right"),
        N_EXPERTS - 1).astype(jnp.int32)
    nvalid = (pend[-1:] // tile).astype(jnp.int32)
    return row_tok, pos.reshape(m, TOP_K), blk_e, nvalid


def _final_kernel(x1_ref, yg_ref, gate_ref, p_ref, wpg_ref, bpg_ref, wpp_ref, g_ref, b_ref, o_ref):
    x1 = x1_ref[...]
    gate = gate_ref[...]
    ffn = gate[:, 0:1] * yg_ref[0]
    for j in range(1, TOP_K):
        ffn = ffn + gate[:, j:j + 1] * yg_ref[j]
    ple = jax.nn.sigmoid(_dot(x1, wpg_ref[...]) + bpg_ref[...]) * _dot(p_ref[...], wpp_ref[...])
    o_ref[...] = _layer_norm(DEEPNORM_ALPHA * x1 + ffn + ple, g_ref[...], b_ref[...], LN_EPS)


def _final(x1, yg, gate, p, wpg, bpg, wpp, g, b, layer):
    m = x1.shape[0]
    tm = TOK_TILE
    return pl.pallas_call(
        _final_kernel, grid=(m // tm,),
        in_specs=[_row_spec(tm, D_MODEL),
                  pl.BlockSpec((TOP_K, tm, D_MODEL), lambda i: (0, i, 0)),
                  _row_spec(tm, LANES),
                  pl.BlockSpec((None, tm, D_PLE), lambda i: (layer, i, 0)),
                  _const_spec((D_MODEL, D_MODEL), layer), _const_spec((1, D_MODEL), layer),
                  _const_spec((D_PLE, D_MODEL), layer), _const_spec((1, D_MODEL), layer),
                  _const_spec((1, D_MODEL), layer)],
        out_specs=_row_spec(tm, D_MODEL),
        out_shape=jax.ShapeDtypeStruct((m, D_MODEL), F32),
        compiler_params=_cparams(), name="final")(x1, yg, gate, p, wpg, bpg, wpp, g, b)


def _pad_lowrank(w):
    z = jnp.zeros_like(w[:, 0])
    fwd = jnp.concatenate([w[:, 0], z], axis=1)
    bwd = jnp.concatenate([z, w[:, 1]], axis=1)
    return jnp.stack([fwd, bwd], axis=1)


def kernel(x_prompt, x_sample, p_prompt, p_sample, ln0_g, ln0_b, w_in, b_in, mu_shift, dw_w, dw_b, cln_g, cln_b, w_cb, b_cb, w0, w2, a0, a2, g2, k_k, k_a, r_k, lnx_g, lnx_b, w_rb, w_out, b_out, ln1_g, ln1_b, w_router, b_router, w_gu, b_gu, w_down, b_down, w_pg, b_pg, w_pp, ln2_g, ln2_b):
    bp, tp, d = x_prompt.shape
    bs, ts, _ = x_sample.shape
    mp, ms = bp * tp, bs * ts
    m = mp + ms
    assert d == D_MODEL and tp % TOK_TILE == 0 and ts % TOK_TILE == 0
    seq = (mp, tp, ts, m)
    L = w_in.shape[0]

    x = jnp.concatenate([x_prompt.reshape(mp, d), x_sample.reshape(ms, d)], axis=0)
    p = jnp.concatenate([p_prompt.reshape(L, mp, D_PLE), p_sample.reshape(L, ms, D_PLE)], axis=1)

    row = lambda a: a.reshape(a.shape[0], 1, -1)
    wc = w_in[:, :, :N_GLU].astype(BF16)
    wz = w_in[:, :, N_GLU:N_GLU + N_SHIFT].astype(BF16)
    wg = w_in[:, :, N_GLU + N_SHIFT:].astype(BF16)
    bc, bz, bg = row(b_in[:, :N_GLU]), row(b_in[:, N_GLU:N_GLU + N_SHIFT]), row(b_in[:, N_GLU + N_SHIFT:])
    w2p = _pad_lowrank(w2).astype(BF16)
    a2p = _pad_lowrank(a2).astype(BF16)
    head = jnp.arange(D_RWKV, dtype=jnp.int32) // RWKV_HEAD
    hs = (head[:, None] == head[None, :]).astype(BF16)
    wr_pad = jnp.pad(w_router, ((0, 0), (0, 0), (0, LANES - N_EXPERTS)))
    wrh = wr_pad.astype(BF16)
    wrl = (wr_pad - wrh.astype(F32)).astype(BF16)
    br = row(jnp.pad(b_router, ((0, 0), (0, LANES - N_EXPERTS))))
    wgu_b = w_gu.astype(BF16)
    wd_b = w_down.astype(BF16)
    bgu = b_gu.reshape(L, N_EXPERTS, 1, 2 * D_EXPERT)
    bdn = b_down.reshape(L, N_EXPERTS, 1, D_MODEL)
    wcb_b, wrb_b, wout_b = w_cb.astype(BF16), w_rb.astype(BF16), w_out.astype(BF16)
    wpg_b, wpp_b, g2_b = w_pg.astype(BF16), w_pp.astype(BF16), g2.astype(BF16)

    x = _ln0(x, ln0_g, ln0_b)
    for l in range(L):
        c, z, gt = _in_proj(x, wc, wz, wg, bc, bz, bg, l)
        conv = _conv_branch(c, dw_w, row(dw_b), row(cln_g), row(cln_b), wcb_b, row(b_cb), l, seq)
        (r, v, kk, g, bonus, lwf, kdf, bf, lwb, kdb, bb) = _rwkv_prep(
            z, mu_shift, w0, w2p, a0, a2p, g2_b, row(k_k), row(k_a), row(r_k.reshape(L, D_RWKV)), hs, l, seq)
        yf = _wkv(r, v, kk, lwf, kdf, bf, False, seq)
        yb = _wkv(r, v, kk, lwb, kdb, bb, True, seq)
        x1, x1b, tope, gate = _post(
            x, yf, yb, bonus, g, conv, gt, hs, row(lnx_g), row(lnx_b), wrb_b, wout_b, row(b_out),
            row(ln1_g), row(ln1_b), wrh, wrl, br, l)
        row_tok, pos, blk_e, nvalid = _route(tope[:, :TOP_K], MOE_TILE)
        y_rows = _moe(blk_e, nvalid, x1b[row_tok], wgu_b, bgu, wd_b, bdn, l)
        yg = y_rows[pos.T]
        x = _final(x1, yg, gate, p, wpg_b, row(b_pg), wpp_b, row(ln2_g), row(ln2_b), l)

    return (x[:mp].reshape(bp, tp, d), x[mp:].reshape(bs, ts, d))
```

```python
import functools
import math

import jax
import jax.numpy as jnp
from jax import lax
from jax.experimental import pallas as pl
from jax.experimental.pallas import tpu as pltpu

F32 = jnp.float32
BF16 = jnp.bfloat16

D_MODEL = 1024
DEPTH = 4
D_CONV = 512
CONV_WIDTH = 31
CONV_HALO = 16
D_RWKV = 512
RWKV_HEAD = 64
DECAY_RANK = 64
ICLR_RANK = 64
GATE_RANK = 128
N_EXPERTS = 32
TOP_K = 4
D_EXPERT = 1024
SWIGLU_LIMIT = 7.0
SWIGLU_ALPHA = 1.702
D_PLE = 256
LN_EPS = 1e-5
GN_EPS = 64e-5
DEEPNORM_ALPHA = (2 * DEPTH) ** 0.25
DECAY_SCALE = math.exp(-0.5)
N_GLU = 2 * D_CONV
N_SHIFT = 3 * D_RWKV + 2 * DECAY_RANK + 2 * ICLR_RANK + GATE_RANK
N_GATES = 2 * D_MODEL
N_LOWRANK = 2 * DECAY_RANK

LANES = 128
SUBLANES = 8
SHIFT_HALO = SUBLANES
TOK_TILE = 256
LN0_TILE = 512
MOE_TILE = 512
WKV_CHUNK = 64
WKV_PAIR = 2
WKV_PW = WKV_PAIR * RWKV_HEAD
WKV_INTER_CHUNKS = 4
VMEM_LIMIT = 48 * 1024 * 1024
assert WKV_CHUNK == RWKV_HEAD


def _cparams(n_axes=1):
    return pltpu.CompilerParams(dimension_semantics=("arbitrary",) * n_axes,
                                vmem_limit_bytes=VMEM_LIMIT)


def _dot(a, b):
    return jnp.dot(a.astype(BF16), b.astype(BF16), preferred_element_type=F32)


def _dot_nt(a, b):
    return lax.dot_general(a.astype(BF16), b.astype(BF16), (((1,), (1,)), ((), ())),
                           preferred_element_type=F32)


def _dot_tn(a, b):
    return lax.dot_general(a.astype(BF16), b.astype(BF16), (((0,), (0,)), ((), ())),
                           preferred_element_type=F32)


def _split_bf16(x):
    hi = x.astype(BF16)
    lo = (x - hi.astype(F32)).astype(BF16)
    return hi, lo


def _dot_hilo(x, w01):
    hi, lo = _split_bf16(x)
    return (jnp.dot(hi, w01, preferred_element_type=F32)
            + jnp.dot(lo, w01, preferred_element_type=F32))


def _layer_norm(x, g, b, eps):
    mu = jnp.mean(x, axis=-1, keepdims=True)
    xc = x - mu
    var = jnp.mean(xc * xc, axis=-1, keepdims=True)
    return xc * lax.rsqrt(var + eps) * g + b


def _seq_start(tok, m_prompt, t_prompt, t_sample, m_total):
    in_prompt = tok < m_prompt
    at_prompt_start = lax.rem(tok, jnp.int32(t_prompt)) == 0
    at_sample_start = lax.rem(jnp.abs(tok - m_prompt), jnp.int32(t_sample)) == 0
    return ((in_prompt & at_prompt_start) | (jnp.logical_not(in_prompt) & at_sample_start)
            | (tok >= m_total))


def _const_spec(shape, layer=None):
    if layer is None:
        nd = len(shape)
        return pl.BlockSpec(shape, lambda i, _n=nd: (0,) * _n)
    nd = len(shape)
    return pl.BlockSpec((None,) + tuple(shape), lambda i, _l=layer, _n=nd: (_l,) + (0,) * _n)


def _row_spec(tm, width):
    return pl.BlockSpec((tm, width), lambda i: (i, 0))


def _ln_kernel(x_ref, g_ref, b_ref, o_ref):
    o_ref[...] = _layer_norm(x_ref[...], g_ref[...], b_ref[...], LN_EPS)


def _ln0(x, g, b):
    m = x.shape[0]
    tm = LN0_TILE
    return pl.pallas_call(
        _ln_kernel, grid=(m // tm,),
        in_specs=[_row_spec(tm, D_MODEL), _const_spec((1, D_MODEL)), _const_spec((1, D_MODEL))],
        out_specs=_row_spec(tm, D_MODEL),
        out_shape=jax.ShapeDtypeStruct((m, D_MODEL), F32),
        compiler_params=_cparams(), name="ln0")(x, g.reshape(1, -1), b.reshape(1, -1))


def _in_proj_kernel(x_ref, wc_ref, wz_ref, wg_ref, bc_ref, bz_ref, bg_ref, c_ref, z_ref, gt_ref):
    xb = x_ref[...].astype(BF16)
    c_ref[...] = jnp.dot(xb, wc_ref[...], preferred_element_type=F32) + bc_ref[...]
    z_ref[...] = jnp.dot(xb, wz_ref[...], preferred_element_type=F32) + bz_ref[...]
    gt_ref[...] = jnp.dot(xb, wg_ref[...], preferred_element_type=F32) + bg_ref[...]


def _in_proj(x, wc, wz, wg, bc, bz, bg, layer):
    m = x.shape[0]
    tm = TOK_TILE
    return pl.pallas_call(
        _in_proj_kernel, grid=(m // tm,),
        in_specs=[_row_spec(tm, D_MODEL),
                  _const_spec((D_MODEL, N_GLU), layer), _const_spec((D_MODEL, N_SHIFT), layer),
                  _const_spec((D_MODEL, N_GATES), layer),
                  _const_spec((1, N_GLU), layer), _const_spec((1, N_SHIFT), layer),
                  _const_spec((1, N_GATES), layer)],
        out_specs=[_row_spec(tm, N_GLU), _row_spec(tm, N_SHIFT), _row_spec(tm, N_GATES)],
        out_shape=[jax.ShapeDtypeStruct((m, N_GLU), F32), jax.ShapeDtypeStruct((m, N_SHIFT), F32),
                   jax.ShapeDtypeStruct((m, N_GATES), F32)],
        compiler_params=_cparams(), name="in_proj")(x, wc, wz, wg, bc, bz, bg)


def _conv_kernel(seq, c_ref, cp_ref, cn_ref, dww_ref, dwb_ref, g_ref, b_ref, wcb_ref, bcb_ref,
                 o_ref, u_ref, sh_ref):
    tm = c_ref.shape[0]
    i = pl.program_id(0)
    first = _seq_start(i * tm, *seq)
    last = _seq_start((i + 1) * tm, *seq)

    def glu(c):
        return c[:, :D_CONV] * jax.nn.sigmoid(c[:, D_CONV:])

    u_ref[0:CONV_HALO, :] = jnp.where(first, 0.0, glu(cp_ref[...]))
    u_ref[CONV_HALO:CONV_HALO + tm, :] = glu(c_ref[...])
    u_ref[CONV_HALO + tm:, :] = jnp.where(last, 0.0, glu(cn_ref[...]))

    span = sh_ref.shape[1]
    for s in range(SUBLANES):
        sh_ref[s] = u_ref[s:s + span, :]
    base = CONV_HALO - CONV_WIDTH // 2
    acc = jnp.zeros((tm, D_CONV), F32) + dwb_ref[...]
    for j in range(CONV_WIDTH):
        a, s = divmod(base + j, SUBLANES)
        acc = acc + sh_ref[s, a * SUBLANES:a * SUBLANES + tm, :] * dww_ref[j:j + 1, :]
    u = _layer_norm(acc, g_ref[...], b_ref[...], LN_EPS)
    u = u * jax.nn.sigmoid(u)
    o_ref[...] = _dot(u, wcb_ref[...]) + bcb_ref[...]


def _conv_branch(c, dww, dwb, g, b, wcb, bcb, layer, seq):
    m = c.shape[0]
    tm = TOK_TILE
    r = tm // CONV_HALO
    nh = m // CONV_HALO
    return pl.pallas_call(
        functools.partial(_conv_kernel, seq), grid=(m // tm,),
        in_specs=[_row_spec(tm, N_GLU),
                  pl.BlockSpec((CONV_HALO, N_GLU), lambda i: (jnp.maximum(i * r - 1, 0), 0)),
                  pl.BlockSpec((CONV_HALO, N_GLU), lambda i: (jnp.minimum((i + 1) * r, nh - 1), 0)),
                  _const_spec((CONV_WIDTH, D_CONV), layer), _const_spec((1, D_CONV), layer),
                  _const_spec((1, D_CONV), layer), _const_spec((1, D_CONV), layer),
                  _const_spec((D_CONV, D_MODEL), layer), _const_spec((1, D_MODEL), layer)],
        out_specs=_row_spec(tm, D_MODEL),
        out_shape=jax.ShapeDtypeStruct((m, D_MODEL), F32),
        scratch_shapes=[pltpu.VMEM((tm + 2 * CONV_HALO, D_CONV), F32),
                        pltpu.VMEM((SUBLANES, tm + 2 * CONV_HALO - SUBLANES, D_CONV), F32)],
        compiler_params=_cparams(), name="conv_branch")(c, c, c, dww, dwb, g, b, wcb, bcb)


def _rwkv_prep_kernel(seq, z_ref, zp_ref, zn_ref, mu_ref, w0_ref, w2_ref, a0_ref, a2_ref, g2_ref,
                      kk_ref, ka_ref, rk_ref, hs_ref,
                      r_ref, v_ref, kkn_ref, g_ref, bonus_ref,
                      lwf_ref, kdf_ref, bf_ref, lwb_ref, kdb_ref, bb_ref):
    tm = z_ref.shape[0]
    i = pl.program_id(0)
    first = _seq_start(i * tm, *seq)
    last = _seq_start((i + 1) * tm, *seq)

    z = z_ref[...]
    row = lax.broadcasted_iota(jnp.int32, (tm, 1), 0)
    prev_row = jnp.where(first, 0.0, zp_ref[SHIFT_HALO - 1:SHIFT_HALO, :])
    next_row = jnp.where(last, 0.0, zn_ref[0:1, :])
    zp = jnp.where(row == 0, prev_row, pltpu.roll(z, 1, 0))
    zn = jnp.where(row == tm - 1, next_row, pltpu.roll(z, tm - 1, 0))
    zs = z + mu_ref[0:1, :] * (zp - z) + mu_ref[1:2, :] * (zn - z)

    o = 3 * D_RWKV
    r = zs[:, 0:D_RWKV]
    k = zs[:, D_RWKV:2 * D_RWKV]
    v = zs[:, 2 * D_RWKV:o]
    lw = zs[:, o:o + N_LOWRANK]
    la = zs[:, o + N_LOWRANK:o + 2 * N_LOWRANK]
    lg = zs[:, o + 2 * N_LOWRANK:]

    hs = hs_ref[...]
    kk = k * kk_ref[...]
    kk = kk * lax.rsqrt(_dot_hilo(kk * kk, hs) + 1e-12)
    r_ref[...] = r
    v_ref[...] = v
    kkn_ref[...] = kk
    g_ref[...] = _dot(jax.nn.sigmoid(lg), g2_ref[...])
    bonus_ref[...] = _dot_hilo(r * k * rk_ref[...], hs) * v

    tlw = jnp.tanh(lw)
    ka = ka_ref[...]
    for d, (lw_o, kd_o, b_o) in enumerate(((lwf_ref, kdf_ref, bf_ref), (lwb_ref, kdb_ref, bb_ref))):
        t = w0_ref[d:d + 1, :] + _dot(tlw, w2_ref[d])
        a = jax.nn.sigmoid(a0_ref[d:d + 1, :] + _dot(la, a2_ref[d]))
        lw_o[...] = -DECAY_SCALE * jax.nn.sigmoid(t)
        kd_o[...] = k * (1.0 + (a - 1.0) * ka)
        b_o[...] = kk * a


def _rwkv_prep(z, mu, w0, w2p, a0, a2p, g2, k_k, k_a, r_k, hs, layer, seq):
    m = z.shape[0]
    tm = TOK_TILE
    r = tm // SHIFT_HALO
    nh = m // SHIFT_HALO
    outs = [jax.ShapeDtypeStruct((m, D_RWKV), F32)] * 11
    return pl.pallas_call(
        functools.partial(_rwkv_prep_kernel, seq), grid=(m // tm,),
        in_specs=[_row_spec(tm, N_SHIFT),
                  pl.BlockSpec((SHIFT_HALO, N_SHIFT), lambda i: (jnp.maximum(i * r - 1, 0), 0)),
                  pl.BlockSpec((SHIFT_HALO, N_SHIFT), lambda i: (jnp.minimum((i + 1) * r, nh - 1), 0)),
                  _const_spec((2, N_SHIFT), layer), _const_spec((2, D_RWKV), layer),
                  _const_spec((2, N_LOWRANK, D_RWKV), layer), _const_spec((2, D_RWKV), layer),
                  _const_spec((2, N_LOWRANK, D_RWKV), layer), _const_spec((GATE_RANK, D_RWKV), layer),
                  _const_spec((1, D_RWKV), layer), _const_spec((1, D_RWKV), layer),
                  _const_spec((1, D_RWKV), layer), _const_spec((D_RWKV, D_RWKV))],
        out_specs=[_row_spec(tm, D_RWKV)] * 11,
        out_shape=outs,
        compiler_params=_cparams(), name="rwkv_prep")(z, z, z, mu, w0, w2p, a0, a2p, g2, k_k, k_a, r_k, hs)


def _tile_rows(x, n):
    return jnp.concatenate([x] * n, axis=0)


def _fold_rows(x, n):
    h = x.shape[0] // n
    acc = x[0:h]
    for j in range(1, n):
        acc = acc + x[j * h:(j + 1) * h]
    return acc


def _same_head_mask():
    rows = lax.broadcasted_iota(jnp.int32, (WKV_PW, WKV_PW), 0)
    cols = lax.broadcasted_iota(jnp.int32, (WKV_PW, WKV_PW), 1)
    head_shift = RWKV_HEAD.bit_length() - 1
    same_head = (rows >> head_shift) == (cols >> head_shift)
    return same_head, rows & (RWKV_HEAD - 1), cols & (RWKV_HEAD - 1), rows == cols


def _wkv_intra_kernel(r_ref, v_ref, kk_ref, lwf_ref, kdf_ref, bf_ref, lwb_ref, kdb_ref, bb_ref,
                      qf_ref, y0f_ref, pf_ref, hf_ref, gf_ref, qb_ref, y0b_ref, pb_ref, hb_ref, gb_ref):
    C, PW, NP = WKV_CHUNK, WKV_PW, WKV_PAIR
    same_head, t_i, s_i, diag = _same_head_mask()
    eye = diag.astype(F32)
    t_c = lax.broadcasted_iota(jnp.int32, (C, C), 0)
    s_c = lax.broadcasted_iota(jnp.int32, (C, C), 1)
    r, v, kk = r_ref[...], v_ref[...], kk_ref[...]

    def bd(x):
        return jnp.where(same_head, _tile_rows(x, NP), 0.0)

    probs = []
    for reverse, (lw_ref, kd_ref, b_ref), out_refs in (
            (False, (lwf_ref, kdf_ref, bf_ref), (qf_ref, y0f_ref, pf_ref, hf_ref, gf_ref)),
            (True, (lwb_ref, kdb_ref, bb_ref), (qb_ref, y0b_ref, pb_ref, hb_ref, gb_ref))):
        cum_mat = ((s_c >= t_c) if reverse else (s_c <= t_c)).astype(BF16)
        earlier = same_head & ((s_i > t_i) if reverse else (s_i < t_i))
        readable = earlier if reverse else (same_head & (s_i <= t_i))
        last_row = 0 if reverse else C - 1

        lw = lw_ref[...]
        lw_hi, lw_lo = _split_bf16(lw)
        cum = (jnp.dot(cum_mat, lw_hi, preferred_element_type=F32)
               + jnp.dot(cum_mat, lw_lo, preferred_element_type=F32))
        g_ex = jnp.exp(cum - lw)
        g_inv = jnp.exp(-cum)
        g_end = jnp.exp(cum[last_row:last_row + 1, :])
        al_all = -kk * g_ex
        be_all = b_ref[...] * g_inv
        kb_all = kd_ref[...] * g_inv
        rb_all = r * (g_ex if reverse else jnp.exp(cum))
        bt_all = be_all * g_end
        kt_all = kb_all * g_end
        out_refs[4][...] = g_end
        for p in range(D_RWKV // PW):
            sl = slice(p * PW, (p + 1) * PW)
            probs.append(dict(sl=sl, out=out_refs, earlier=earlier, readable=readable,
                              al=al_all[:, sl], be=be_all[:, sl], kb=kb_all[:, sl], rb=rb_all[:, sl],
                              v=v[:, sl], bt=bt_all[:, sl], kt=kt_all[:, sl]))

    for c in probs:
        c["al_bd"] = bd(c["al"])
        aa = _dot_nt(jnp.concatenate([c["al_bd"], bd(c["rb"])], axis=0),
                     jnp.concatenate([_tile_rows(c["be"], NP), _tile_rows(c["kb"], NP)], axis=0))
        c["a_ab"] = jnp.where(c["earlier"], aa[:PW, :PW], 0.0)
        a_ak = jnp.where(c["earlier"], aa[:PW, PW:], 0.0)
        c["a_rb"] = jnp.where(c["readable"], aa[PW:, :PW], 0.0)
        a_rk = jnp.where(c["readable"], aa[PW:, PW:], 0.0)
        c["a_kk"] = jnp.concatenate([a_ak, a_rk], axis=0)
    for c in probs:
        c["akv"] = _dot(c["a_kk"], bd(c["v"]))
    for c in probs:
        c["inv"] = eye + c["a_ab"]
        c["pk"] = _dot(c["a_ab"], c["a_ab"])
    for _ in range(C.bit_length() - 3):
        for c in probs:
            sq_and_apply = _dot(c["pk"], jnp.concatenate([c["pk"], c["inv"]], axis=1))
            c["inv"] = c["inv"] + sq_and_apply[:, PW:]
            c["pk"] = sq_and_apply[:, :PW]
    for c in probs:
        c["inv"] = c["inv"] + _dot(c["pk"], c["inv"])
    for c in probs:
        c["wu"] = _dot(c["inv"], jnp.concatenate([c["al_bd"], c["akv"][:PW]], axis=1))
    for c in probs:
        c["qy"] = _dot(c["a_rb"], c["wu"])
    for c in probs:
        q_ref, y0_ref, p_ref, h_ref, _ = c["out"]
        sl, bt = c["sl"], c["bt"]
        q_ref[:, sl] = (c["rb"] + _fold_rows(c["qy"][:, :PW], NP)).astype(BF16)
        y0_ref[:, sl] = _fold_rows(c["qy"][:, PW:] + c["akv"][PW:], NP)
        w = _fold_rows(c["wu"][:, :PW], NP)
        u0 = _fold_rows(c["wu"][:, PW:], NP)
        pm = _dot_tn(w, bt)
        hm = _dot_tn(jnp.concatenate([u0, c["v"]], axis=0), jnp.concatenate([bt, c["kt"]], axis=0))
        p_ref[:, sl] = _fold_rows(jnp.where(same_head, pm, 0.0), NP).astype(BF16)
        h_ref[:, sl] = _fold_rows(jnp.where(same_head, hm, 0.0), NP)


def _wkv_intra(r, v, kk, lwf, kdf, bf, lwb, kdb, bb):
    m = r.shape[0]
    n_chunks = m // WKV_CHUNK
    spec = pl.BlockSpec((WKV_CHUNK, D_RWKV), lambda i: (i, 0))
    gspec = pl.BlockSpec((None, 1, D_RWKV), lambda i: (i, 0, 0))
    f32_tok = jax.ShapeDtypeStruct((m, D_RWKV), F32)
    bf_tok = jax.ShapeDtypeStruct((m, D_RWKV), BF16)
    g_shape = jax.ShapeDtypeStruct((n_chunks, 1, D_RWKV), F32)
    return pl.pallas_call(
        _wkv_intra_kernel, grid=(n_chunks,),
        in_specs=[spec] * 9,
        out_specs=[spec, spec, spec, spec, gspec] * 2,
        out_shape=[bf_tok, f32_tok, bf_tok, f32_tok, g_shape] * 2,
        compiler_params=_cparams(), name="wkv_intra")(r, v, kk, lwf, kdf, bf, lwb, kdb, bb)


def _wkv_inter_kernel(seq, qf_ref, y0f_ref, pf_ref, hf_ref, gf_ref, qb_ref, y0b_ref, pb_ref, hb_ref,
                      gb_ref, yf_ref, yb_ref, s_ref):
    C, PW, NP, NB = WKV_CHUNK, WKV_PW, WKV_PAIR, WKV_INTER_CHUNKS
    n_steps = pl.num_programs(0)
    i = pl.program_id(0)
    same_head = _same_head_mask()[0]

    @pl.when(i == 0)
    def _():
        s_ref[...] = jnp.zeros_like(s_ref)

    dirs = ((qf_ref, y0f_ref, pf_ref, hf_ref, gf_ref, yf_ref),
            (qb_ref, y0b_ref, pb_ref, hb_ref, gb_ref, yb_ref))
    for jj in range(NB):
        chains = []
        for d, (q_ref, y0_ref, p_ref, h_ref, g_ref, y_ref) in enumerate(dirs):
            reverse = d == 1
            blk = (n_steps - 1 - i) if reverse else i
            j = NB - 1 - jj if reverse else jj
            chunk = blk * NB + j
            fresh = _seq_start((chunk + 1) * C if reverse else chunk * C, *seq)
            rs = slice(j * C, (j + 1) * C)
            g = g_ref[j]
            for p in range(D_RWKV // PW):
                sl = slice(p * PW, (p + 1) * PW)
                s0 = jnp.where(fresh, 0.0, s_ref[d, p])
                pm = jnp.where(same_head, _tile_rows(p_ref[rs, sl], NP), 0.0)
                chains.append((d, p, rs, sl, s0, _dot(s0, pm), g[:, sl]))
        for d, p, rs, sl, s0, s0_p, g in chains:
            q_ref, y0_ref, _, h_ref, _, y_ref = dirs[d]
            y_ref[rs, sl] = _dot_nt(q_ref[rs, sl], s0) + y0_ref[rs, sl]
            hm = jnp.where(same_head, _tile_rows(h_ref[rs, sl], NP), 0.0)
            s_ref[d, p] = s0 * g + s0_p + hm


def _wkv_inter(qf, y0f, pf, hf, gf, qb, y0b, pb, hb, gb, seq):
    m = qf.shape[0]
    nb = WKV_INTER_CHUNKS
    tb = nb * WKV_CHUNK
    n_steps = m // tb
    fwd = pl.BlockSpec((tb, D_RWKV), lambda i: (i, 0))
    bwd = pl.BlockSpec((tb, D_RWKV), lambda i: (n_steps - 1 - i, 0))
    gfwd = pl.BlockSpec((nb, 1, D_RWKV), lambda i: (i, 0, 0))
    gbwd = pl.BlockSpec((nb, 1, D_RWKV), lambda i: (n_steps - 1 - i, 0, 0))
    out = jax.ShapeDtypeStruct((m, D_RWKV), F32)
    return pl.pallas_call(
        functools.partial(_wkv_inter_kernel, seq), grid=(n_steps,),
        in_specs=[fwd] * 4 + [gfwd] + [bwd] * 4 + [gbwd],
        out_specs=[fwd, bwd], out_shape=[out, out],
        scratch_shapes=[pltpu.VMEM((2, D_RWKV // WKV_PW, WKV_PW, WKV_PW), F32)],
        compiler_params=_cparams(), name="wkv_inter")(qf, y0f, pf, hf, gf, qb, y0b, pb, hb, gb)


def _post_kernel(x_ref, yf_ref, yb_ref, bonus_ref, g_ref, conv_ref, gt_ref, hs_ref,
                 lnxg_ref, lnxb_ref, wrb_ref, wout_ref, bout_ref, ln1g_ref, ln1b_ref,
                 wrh_ref, wrl_ref, br_ref,
                 x1_ref, x1b_ref, tope_ref, gate_ref):
    hs = hs_ref[...]
    y = yf_ref[...] + yb_ref[...]
    inv_n = 1.0 / RWKV_HEAD
    mu = _dot_hilo(y, hs) * inv_n
    yc = y - mu
    var = _dot_hilo(yc * yc, hs) * inv_n
    y = yc * lax.rsqrt(var + GN_EPS) * lnxg_ref[...] + lnxb_ref[...]
    y = (y + bonus_ref[...]) * g_ref[...]
    rwkv_out = _dot(y, wrb_ref[...])
    gt = gt_ref[...]
    mixed = (jax.nn.sigmoid(gt[:, :D_MODEL]) * conv_ref[...]
             + jax.nn.sigmoid(gt[:, D_MODEL:]) * rwkv_out)
    mix = _dot(mixed, wout_ref[...]) + bout_ref[...]
    x1 = _layer_norm(DEEPNORM_ALPHA * x_ref[...] + mix, ln1g_ref[...], ln1b_ref[...], LN_EPS)
    x1_ref[...] = x1
    x1b_ref[...] = x1.astype(BF16)

    xh, xl = _split_bf16(x1)
    logits = (jnp.dot(xh, wrh_ref[...], preferred_element_type=F32)
              + jnp.dot(xh, wrl_ref[...], preferred_element_type=F32)
              + jnp.dot(xl, wrh_ref[...], preferred_element_type=F32)) + br_ref[...]
    lane = lax.broadcasted_iota(jnp.int32, logits.shape, 1).astype(F32)
    neg = jnp.float32(-jnp.inf)
    logits = jnp.where(lane < N_EXPERTS, logits, neg)
    tops, idxs = [], []
    for _ in range(TOP_K):
        mx = jnp.max(logits, axis=-1, keepdims=True)
        ix = jnp.min(jnp.where(logits == mx, lane, float(LANES)), axis=-1, keepdims=True)
        tops.append(mx)
        idxs.append(ix)
        logits = jnp.where(lane == ix, neg, logits)
    es = [jnp.exp(t - tops[0]) for t in tops]
    denom = es[0] + es[1] + es[2] + es[3]
    tope = jnp.zeros(logits.shape, F32)
    gate = jnp.zeros(logits.shape, F32)
    for j in range(TOP_K):
        tope = jnp.where(lane == j, idxs[j], tope)
        gate = jnp.where(lane == j, es[j] / denom, gate)
    tope_ref[...] = tope.astype(jnp.int32)
    gate_ref[...] = gate


def _post(x, yf, yb, bonus, g, conv, gt, hs, lnxg, lnxb, wrb, wout, bout, ln1g, ln1b,
          wrh, wrl, br, layer):
    m = x.shape[0]
    tm = TOK_TILE
    return pl.pallas_call(
        _post_kernel, grid=(m // tm,),
        in_specs=[_row_spec(tm, D_MODEL)] + [_row_spec(tm, D_RWKV)] * 4
                 + [_row_spec(tm, D_MODEL), _row_spec(tm, N_GATES), _const_spec((D_RWKV, D_RWKV)),
                    _const_spec((1, D_RWKV), layer), _const_spec((1, D_RWKV), layer),
                    _const_spec((D_RWKV, D_MODEL), layer), _const_spec((D_MODEL, D_MODEL), layer),
                    _const_spec((1, D_MODEL), layer), _const_spec((1, D_MODEL), layer),
                    _const_spec((1, D_MODEL), layer),
                    _const_spec((D_MODEL, LANES), layer), _const_spec((D_MODEL, LANES), layer),
                    _const_spec((1, LANES), layer)],
        out_specs=[_row_spec(tm, D_MODEL), _row_spec(tm, D_MODEL), _row_spec(tm, LANES),
                   _row_spec(tm, LANES)],
        out_shape=[jax.ShapeDtypeStruct((m, D_MODEL), F32), jax.ShapeDtypeStruct((m, D_MODEL), BF16),
                   jax.ShapeDtypeStruct((m, LANES), jnp.int32), jax.ShapeDtypeStruct((m, LANES), F32)],
        compiler_params=_cparams(), name="post")(
            x, yf, yb, bonus, g, conv, gt, hs, lnxg, lnxb, wrb, wout, bout, ln1g, ln1b, wrh, wrl, br)


def _moe_kernel(blk_e_ref, nvalid_ref, x_ref, wgu_ref, bgu_ref, wd_ref, bd_ref, o_ref):
    i = pl.program_id(0)

    @pl.when(i < nvalid_ref[0])
    def _():
        hgu = jnp.dot(x_ref[...], wgu_ref[...], preferred_element_type=F32) + bgu_ref[...]
        hg = jnp.minimum(hgu[:, :D_EXPERT], SWIGLU_LIMIT)
        hl = jnp.clip(hgu[:, D_EXPERT:], -SWIGLU_LIMIT, SWIGLU_LIMIT)
        act = hg * jax.nn.sigmoid(SWIGLU_ALPHA * hg) * (hl + 1.0)
        o_ref[...] = _dot(act, wd_ref[...]) + bd_ref[...]

    @pl.when(i >= nvalid_ref[0])
    def _():
        o_ref[...] = jnp.zeros_like(o_ref)


def _moe(blk_e, nvalid, xs, wgu, bgu, wd, bd, layer):
    n_rows = xs.shape[0]
    tm = MOE_TILE
    grid_spec = pltpu.PrefetchScalarGridSpec(
        num_scalar_prefetch=2, grid=(n_rows // tm,),
        in_specs=[pl.BlockSpec((tm, D_MODEL), lambda i, be, nv: (i, 0)),
                  pl.BlockSpec((None, None, D_MODEL, 2 * D_EXPERT), lambda i, be, nv: (layer, be[i], 0, 0)),
                  pl.BlockSpec((None, None, 1, 2 * D_EXPERT), lambda i, be, nv: (layer, be[i], 0, 0)),
                  pl.BlockSpec((None, None, D_EXPERT, D_MODEL), lambda i, be, nv: (layer, be[i], 0, 0)),
                  pl.BlockSpec((None, None, 1, D_MODEL), lambda i, be, nv: (layer, be[i], 0, 0))],
        out_specs=pl.BlockSpec((tm, D_MODEL), lambda i, be, nv: (i, 0)))
    return pl.pallas_call(
        _moe_kernel, grid_spec=grid_spec,
        out_shape=jax.ShapeDtypeStruct((n_rows, D_MODEL), F32),
        compiler_params=_cparams(), name="moe")(blk_e, nvalid, xs, wgu, bgu, wd, bd)


def _route(top_e, tile):
    m = top_e.shape[0]
    n_assign = m * TOP_K
    flat_e = top_e.reshape(-1)
    iota = jnp.arange(n_assign, dtype=jnp.int32)
    experts = jnp.arange(N_EXPERTS, dtype=jnp.int32)
    _, order = lax.sort((flat_e, iota), num_keys=1, is_stable=True)
    _, rank = lax.sort((order, iota), num_keys=1)
    is_e = flat_e[:, None] == experts[None, :]
    counts = jnp.sum(is_e, axis=0, dtype=jnp.int32)
    start = jnp.cumsum(counts) - counts
    padded = (counts + tile - 1) // tile * tile
    pend = jnp.cumsum(padded)
    shift = pend - padded - start
    pos = rank + jnp.sum(jnp.where(is_e, shift[None, :], 0), axis=1)
    n_blocks = n_assign // tile + N_EXPERTS
    blk_first = jnp.arange(n_blocks, dtype=jnp.int32) * tile
    blk_e = jnp.minimum(jnp.sum(pend[None, :] <= blk_first[:, None], axis=1, dtype=jnp.int32),
                        N_EXPERTS - 1)
    src = (blk_first - shift[blk_e])[:, None] + jnp.arange(tile, dtype=jnp.int32)[None, :]
    valid = src < (start + counts)[blk_e][:, None]
    row_tok = jnp.where(valid, order[jnp.clip(src, 0, n_assign - 1)] // TOP_K, 0).reshape(-1)
    nvalid = (pend[-1:] // tile).astype(jnp.int32)
    return row_tok, pos.reshape(m, TOP_K), blk_e, nvalid


def _final_kernel(x1_ref, yg_ref, gate_ref, p_ref, wpg_ref, bpg_ref, wpp_ref, g_ref, b_ref, o_ref):
    x1 = x1_ref[...]
    gate = gate_ref[...]
    ffn = gate[:, 0:1] * yg_ref[0]
    for j in range(1, TOP_K):
        ffn = ffn + gate[:, j:j + 1] * yg_ref[j]
    ple = jax.nn.sigmoid(_dot(x1, wpg_ref[...]) + bpg_ref[...]) * _dot(p_ref[...], wpp_ref[...])
    o_ref[...] = _layer_norm(DEEPNORM_ALPHA * x1 + ffn + ple, g_ref[...], b_ref[...], LN_EPS)


def _final(x1, yg, gate, p, wpg, bpg, wpp, g, b, layer):
    m = x1.shape[0]
    tm = TOK_TILE
    return pl.pallas_call(
        _final_kernel, grid=(m // tm,),
        in_specs=[_row_spec(tm, D_MODEL),
                  pl.BlockSpec((TOP_K, tm, D_MODEL), lambda i: (0, i, 0)),
                  _row_spec(tm, LANES),
                  pl.BlockSpec((None, tm, D_PLE), lambda i: (layer, i, 0)),
                  _const_spec((D_MODEL, D_MODEL), layer), _const_spec((1, D_MODEL), layer),
                  _const_spec((D_PLE, D_MODEL), layer), _const_spec((1, D_MODEL), layer),
                  _const_spec((1, D_MODEL), layer)],
        out_specs=_row_spec(tm, D_MODEL),
        out_shape=jax.ShapeDtypeStruct((m, D_MODEL), F32),
        compiler_params=_cparams(), name="final")(x1, yg, gate, p, wpg, bpg, wpp, g, b)


def _pad_lowrank(w):
    z = jnp.zeros_like(w[:, 0])
    fwd = jnp.concatenate([w[:, 0], z], axis=1)
    bwd = jnp.concatenate([z, w[:, 1]], axis=1)
    return jnp.stack([fwd, bwd], axis=1)


def kernel(x_prompt, x_sample, p_prompt, p_sample, ln0_g, ln0_b, w_in, b_in, mu_shift, dw_w, dw_b, cln_g, cln_b, w_cb, b_cb, w0, w2, a0, a2, g2, k_k, k_a, r_k, lnx_g, lnx_b, w_rb, w_out, b_out, ln1_g, ln1_b, w_router, b_router, w_gu, b_gu, w_down, b_down, w_pg, b_pg, w_pp, ln2_g, ln2_b):
    bp, tp, d = x_prompt.shape
    bs, ts, _ = x_sample.shape
    mp, ms = bp * tp, bs * ts
    m = mp + ms
    assert d == D_MODEL and tp % TOK_TILE == 0 and ts % TOK_TILE == 0
    assert m % LN0_TILE == 0 and (m * TOP_K) % MOE_TILE == 0
    seq = (mp, tp, ts, m)
    L = w_in.shape[0]

    x = jnp.concatenate([x_prompt.reshape(mp, d), x_sample.reshape(ms, d)], axis=0)
    p = jnp.concatenate([p_prompt.reshape(L, mp, D_PLE), p_sample.reshape(L, ms, D_PLE)], axis=1)

    row = lambda a: a.reshape(a.shape[0], 1, -1)
    wc = w_in[:, :, :N_GLU].astype(BF16)
    wz = w_in[:, :, N_GLU:N_GLU + N_SHIFT].astype(BF16)
    wg = w_in[:, :, N_GLU + N_SHIFT:].astype(BF16)
    bc, bz, bg = row(b_in[:, :N_GLU]), row(b_in[:, N_GLU:N_GLU + N_SHIFT]), row(b_in[:, N_GLU + N_SHIFT:])
    w2p = _pad_lowrank(w2).astype(BF16)
    a2p = _pad_lowrank(a2).astype(BF16)
    head = jnp.arange(D_RWKV, dtype=jnp.int32) // RWKV_HEAD
    hs = (head[:, None] == head[None, :]).astype(BF16)
    wr_pad = jnp.pad(w_router, ((0, 0), (0, 0), (0, LANES - N_EXPERTS)))
    wrh = wr_pad.astype(BF16)
    wrl = (wr_pad - wrh.astype(F32)).astype(BF16)
    br = row(jnp.pad(b_router, ((0, 0), (0, LANES - N_EXPERTS))))
    wgu_b = w_gu.astype(BF16)
    wd_b = w_down.astype(BF16)
    bgu = b_gu.reshape(L, N_EXPERTS, 1, 2 * D_EXPERT)
    bdn = b_down.reshape(L, N_EXPERTS, 1, D_MODEL)
    wcb_b, wrb_b, wout_b = w_cb.astype(BF16), w_rb.astype(BF16), w_out.astype(BF16)
    wpg_b, wpp_b, g2_b = w_pg.astype(BF16), w_pp.astype(BF16), g2.astype(BF16)

    x = _ln0(x, ln0_g, ln0_b)
    for l in range(L):
        c, z, gt = _in_proj(x, wc, wz, wg, bc, bz, bg, l)
        conv = _conv_branch(c, dw_w, row(dw_b), row(cln_g), row(cln_b), wcb_b, row(b_cb), l, seq)
        (r, v, kk, g, bonus, lwf, kdf, bf, lwb, kdb, bb) = _rwkv_prep(
            z, mu_shift, w0, w2p, a0, a2p, g2_b, row(k_k), row(k_a), row(r_k.reshape(L, D_RWKV)), hs, l, seq)
        yf, yb = _wkv_inter(*_wkv_intra(r, v, kk, lwf, kdf, bf, lwb, kdb, bb), seq)
        x1, x1b, tope, gate = _post(
            x, yf, yb, bonus, g, conv, gt, hs, row(lnx_g), row(lnx_b), wrb_b, wout_b, row(b_out),
            row(ln1_g), row(ln1_b), wrh, wrl, br, l)
        row_tok, pos, blk_e, nvalid = _route(tope[:, :TOP_K], MOE_TILE)
        y_rows = _moe(blk_e, nvalid, x1b[row_tok], wgu_b, bgu, wd_b, bdn, l)
        yg = y_rows[pos.T]
        x = _final(x1, yg, gate, p, wpg_b, row(b_pg), wpp_b, row(ln2_g), row(ln2_b), l)

    return (x[:mp].reshape(bp, tp, d), x[mp:].reshape(bs, ts, d))
```

```python
import functools
import math

import jax
import jax.numpy as jnp
from jax import lax
from jax.experimental import pallas as pl
from jax.experimental.pallas import tpu as pltpu

F32 = jnp.float32
BF16 = jnp.bfloat16

D_MODEL = 1024
DEPTH = 4
D_CONV = 512
CONV_WIDTH = 31
CONV_HALO = 16
D_RWKV = 512
RWKV_HEAD = 64
DECAY_RANK = 64
ICLR_RANK = 64
GATE_RANK = 128
N_EXPERTS = 32
TOP_K = 4
D_EXPERT = 1024
SWIGLU_LIMIT = 7.0
SWIGLU_ALPHA = 1.702
D_PLE = 256
LN_EPS = 1e-5
GN_EPS = 64e-5
DEEPNORM_ALPHA = (2 * DEPTH) ** 0.25
DECAY_SCALE = math.exp(-0.5)
N_GLU = 2 * D_CONV
N_SHIFT = 3 * D_RWKV + 2 * DECAY_RANK + 2 * ICLR_RANK + GATE_RANK
N_GATES = 2 * D_MODEL
N_LOWRANK = 2 * DECAY_RANK

LANES = 128
SUBLANES = 8
SHIFT_HALO = SUBLANES
TOK_TILE = 256
LN0_TILE = 512
MOE_TILE = 512
WKV_CHUNK = 64
WKV_PAIR = 2
WKV_PW = WKV_PAIR * RWKV_HEAD
WKV_INTER_CHUNKS = 4
WKV_INTRA_CHUNKS = 4
VMEM_LIMIT = 48 * 1024 * 1024
assert WKV_CHUNK == RWKV_HEAD


def _cparams(n_axes=1):
    return pltpu.CompilerParams(dimension_semantics=("arbitrary",) * n_axes,
                                vmem_limit_bytes=VMEM_LIMIT)


def _dot(a, b):
    return jnp.dot(a.astype(BF16), b.astype(BF16), preferred_element_type=F32)


def _dot_nt(a, b):
    return lax.dot_general(a.astype(BF16), b.astype(BF16), (((1,), (1,)), ((), ())),
                           preferred_element_type=F32)


def _dot_tn(a, b):
    return lax.dot_general(a.astype(BF16), b.astype(BF16), (((0,), (0,)), ((), ())),
                           preferred_element_type=F32)


def _split_bf16(x):
    hi = x.astype(BF16)
    lo = (x - hi.astype(F32)).astype(BF16)
    return hi, lo


def _dot_hilo(x, w01):
    hi, lo = _split_bf16(x)
    return (jnp.dot(hi, w01, preferred_element_type=F32)
            + jnp.dot(lo, w01, preferred_element_type=F32))


def _layer_norm(x, g, b, eps):
    mu = jnp.mean(x, axis=-1, keepdims=True)
    xc = x - mu
    var = jnp.mean(xc * xc, axis=-1, keepdims=True)
    return xc * lax.rsqrt(var + eps) * g + b


def _seq_start(tok, m_prompt, t_prompt, t_sample, m_total):
    in_prompt = tok < m_prompt
    at_prompt_start = lax.rem(tok, jnp.int32(t_prompt)) == 0
    at_sample_start = lax.rem(jnp.abs(tok - m_prompt), jnp.int32(t_sample)) == 0
    return ((in_prompt & at_prompt_start) | (jnp.logical_not(in_prompt) & at_sample_start)
            | (tok >= m_total))


def _const_spec(shape, layer=None):
    if layer is None:
        nd = len(shape)
        return pl.BlockSpec(shape, lambda i, _n=nd: (0,) * _n)
    nd = len(shape)
    return pl.BlockSpec((None,) + tuple(shape), lambda i, _l=layer, _n=nd: (_l,) + (0,) * _n)


def _row_spec(tm, width):
    return pl.BlockSpec((tm, width), lambda i: (i, 0))


def _ln_kernel(x_ref, g_ref, b_ref, o_ref):
    o_ref[...] = _layer_norm(x_ref[...], g_ref[...], b_ref[...], LN_EPS)


def _ln0(x, g, b):
    m = x.shape[0]
    tm = LN0_TILE
    return pl.pallas_call(
        _ln_kernel, grid=(m // tm,),
        in_specs=[_row_spec(tm, D_MODEL), _const_spec((1, D_MODEL)), _const_spec((1, D_MODEL))],
        out_specs=_row_spec(tm, D_MODEL),
        out_shape=jax.ShapeDtypeStruct((m, D_MODEL), F32),
        compiler_params=_cparams(), name="ln0")(x, g.reshape(1, -1), b.reshape(1, -1))


def _in_proj_kernel(x_ref, wc_ref, wz_ref, wg_ref, bc_ref, bz_ref, bg_ref, c_ref, z_ref, gt_ref):
    xb = x_ref[...].astype(BF16)
    c_ref[...] = jnp.dot(xb, wc_ref[...], preferred_element_type=F32) + bc_ref[...]
    z_ref[...] = jnp.dot(xb, wz_ref[...], preferred_element_type=F32) + bz_ref[...]
    gt_ref[...] = jnp.dot(xb, wg_ref[...], preferred_element_type=F32) + bg_ref[...]


def _in_proj(x, wc, wz, wg, bc, bz, bg, layer):
    m = x.shape[0]
    tm = TOK_TILE
    return pl.pallas_call(
        _in_proj_kernel, grid=(m // tm,),
        in_specs=[_row_spec(tm, D_MODEL),
                  _const_spec((D_MODEL, N_GLU), layer), _const_spec((D_MODEL, N_SHIFT), layer),
                  _const_spec((D_MODEL, N_GATES), layer),
                  _const_spec((1, N_GLU), layer), _const_spec((1, N_SHIFT), layer),
                  _const_spec((1, N_GATES), layer)],
        out_specs=[_row_spec(tm, N_GLU), _row_spec(tm, N_SHIFT), _row_spec(tm, N_GATES)],
        out_shape=[jax.ShapeDtypeStruct((m, N_GLU), F32), jax.ShapeDtypeStruct((m, N_SHIFT), F32),
                   jax.ShapeDtypeStruct((m, N_GATES), F32)],
        compiler_params=_cparams(), name="in_proj")(x, wc, wz, wg, bc, bz, bg)


def _conv_kernel(seq, c_ref, cp_ref, cn_ref, dww_ref, dwb_ref, g_ref, b_ref, wcb_ref, bcb_ref,
                 o_ref, u_ref, sh_ref):
    tm = c_ref.shape[0]
    i = pl.program_id(0)
    first = _seq_start(i * tm, *seq)
    last = _seq_start((i + 1) * tm, *seq)

    def glu(c):
        return c[:, :D_CONV] * jax.nn.sigmoid(c[:, D_CONV:])

    u_ref[0:CONV_HALO, :] = jnp.where(first, 0.0, glu(cp_ref[...]))
    u_ref[CONV_HALO:CONV_HALO + tm, :] = glu(c_ref[...])
    u_ref[CONV_HALO + tm:, :] = jnp.where(last, 0.0, glu(cn_ref[...]))

    span = sh_ref.shape[1]
    for s in range(SUBLANES):
        sh_ref[s] = u_ref[s:s + span, :]
    base = CONV_HALO - CONV_WIDTH // 2
    acc = jnp.zeros((tm, D_CONV), F32) + dwb_ref[...]
    for j in range(CONV_WIDTH):
        a, s = divmod(base + j, SUBLANES)
        acc = acc + sh_ref[s, a * SUBLANES:a * SUBLANES + tm, :] * dww_ref[j:j + 1, :]
    u = _layer_norm(acc, g_ref[...], b_ref[...], LN_EPS)
    u = u * jax.nn.sigmoid(u)
    o_ref[...] = _dot(u, wcb_ref[...]) + bcb_ref[...]


def _conv_branch(c, dww, dwb, g, b, wcb, bcb, layer, seq):
    m = c.shape[0]
    tm = TOK_TILE
    r = tm // CONV_HALO
    nh = m // CONV_HALO
    return pl.pallas_call(
        functools.partial(_conv_kernel, seq), grid=(m // tm,),
        in_specs=[_row_spec(tm, N_GLU),
                  pl.BlockSpec((CONV_HALO, N_GLU), lambda i: (jnp.maximum(i * r - 1, 0), 0)),
                  pl.BlockSpec((CONV_HALO, N_GLU), lambda i: (jnp.minimum((i + 1) * r, nh - 1), 0)),
                  _const_spec((CONV_WIDTH, D_CONV), layer), _const_spec((1, D_CONV), layer),
                  _const_spec((1, D_CONV), layer), _const_spec((1, D_CONV), layer),
                  _const_spec((D_CONV, D_MODEL), layer), _const_spec((1, D_MODEL), layer)],
        out_specs=_row_spec(tm, D_MODEL),
        out_shape=jax.ShapeDtypeStruct((m, D_MODEL), F32),
        scratch_shapes=[pltpu.VMEM((tm + 2 * CONV_HALO, D_CONV), F32),
                        pltpu.VMEM((SUBLANES, tm + 2 * CONV_HALO - SUBLANES, D_CONV), F32)],
        compiler_params=_cparams(), name="conv_branch")(c, c, c, dww, dwb, g, b, wcb, bcb)


def _rwkv_prep_kernel(seq, z_ref, zp_ref, zn_ref, mu_ref, w0_ref, w2_ref, a0_ref, a2_ref, g2_ref,
                      kk_ref, ka_ref, rk_ref, hs_ref,
                      r_ref, v_ref, kkn_ref, g_ref, bonus_ref,
                      lwf_ref, kdf_ref, bf_ref, lwb_ref, kdb_ref, bb_ref):
    tm = z_ref.shape[0]
    i = pl.program_id(0)
    first = _seq_start(i * tm, *seq)
    last = _seq_start((i + 1) * tm, *seq)

    z = z_ref[...]
    row = lax.broadcasted_iota(jnp.int32, (tm, 1), 0)
    prev_row = jnp.where(first, 0.0, zp_ref[SHIFT_HALO - 1:SHIFT_HALO, :])
    next_row = jnp.where(last, 0.0, zn_ref[0:1, :])
    zp = jnp.where(row == 0, prev_row, pltpu.roll(z, 1, 0))
    zn = jnp.where(row == tm - 1, next_row, pltpu.roll(z, tm - 1, 0))
    zs = z + mu_ref[0:1, :] * (zp - z) + mu_ref[1:2, :] * (zn - z)

    o = 3 * D_RWKV
    r = zs[:, 0:D_RWKV]
    k = zs[:, D_RWKV:2 * D_RWKV]
    v = zs[:, 2 * D_RWKV:o]
    lw = zs[:, o:o + N_LOWRANK]
    la = zs[:, o + N_LOWRANK:o + 2 * N_LOWRANK]
    lg = zs[:, o + 2 * N_LOWRANK:]

    hs = hs_ref[...]
    kk = k * kk_ref[...]
    kk = kk * lax.rsqrt(_dot_hilo(kk * kk, hs) + 1e-12)
    r_ref[...] = r.astype(BF16)
    v_ref[...] = v.astype(BF16)
    kkn_ref[...] = kk.astype(BF16)
    g_ref[...] = _dot(jax.nn.sigmoid(lg), g2_ref[...]).astype(BF16)
    bonus_ref[...] = (_dot_hilo(r * k * rk_ref[...], hs) * v).astype(BF16)

    tlw = jnp.tanh(lw)
    ka = ka_ref[...]
    for d, (lw_o, kd_o, b_o) in enumerate(((lwf_ref, kdf_ref, bf_ref), (lwb_ref, kdb_ref, bb_ref))):
        t = w0_ref[d:d + 1, :] + _dot(tlw, w2_ref[d])
        a = jax.nn.sigmoid(a0_ref[d:d + 1, :] + _dot(la, a2_ref[d]))
        lw_o[...] = -DECAY_SCALE * jax.nn.sigmoid(t)
        kd_o[...] = (k * (1.0 + (a - 1.0) * ka)).astype(BF16)
        b_o[...] = (kk * a).astype(BF16)


def _rwkv_prep(z, mu, w0, w2p, a0, a2p, g2, k_k, k_a, r_k, hs, layer, seq):
    m = z.shape[0]
    tm = TOK_TILE
    r = tm // SHIFT_HALO
    nh = m // SHIFT_HALO
    f32_tok = jax.ShapeDtypeStruct((m, D_RWKV), F32)
    bf_tok = jax.ShapeDtypeStruct((m, D_RWKV), BF16)
    outs = [bf_tok] * 5 + [f32_tok, bf_tok, bf_tok] * 2
    return pl.pallas_call(
        functools.partial(_rwkv_prep_kernel, seq), grid=(m // tm,),
        in_specs=[_row_spec(tm, N_SHIFT),
                  pl.BlockSpec((SHIFT_HALO, N_SHIFT), lambda i: (jnp.maximum(i * r - 1, 0), 0)),
                  pl.BlockSpec((SHIFT_HALO, N_SHIFT), lambda i: (jnp.minimum((i + 1) * r, nh - 1), 0)),
                  _const_spec((2, N_SHIFT), layer), _const_spec((2, D_RWKV), layer),
                  _const_spec((2, N_LOWRANK, D_RWKV), layer), _const_spec((2, D_RWKV), layer),
                  _const_spec((2, N_LOWRANK, D_RWKV), layer), _const_spec((GATE_RANK, D_RWKV), layer),
                  _const_spec((1, D_RWKV), layer), _const_spec((1, D_RWKV), layer),
                  _const_spec((1, D_RWKV), layer), _const_spec((D_RWKV, D_RWKV))],
        out_specs=[_row_spec(tm, D_RWKV)] * 11,
        out_shape=outs,
        compiler_params=_cparams(), name="rwkv_prep")(z, z, z, mu, w0, w2p, a0, a2p, g2, k_k, k_a, r_k, hs)


def _tile_rows(x, n):
    return jnp.concatenate([x] * n, axis=0)


def _fold_rows(x, n):
    h = x.shape[0] // n
    acc = x[0:h]
    for j in range(1, n):
        acc = acc + x[j * h:(j + 1) * h]
    return acc


def _same_head_mask():
    rows = lax.broadcasted_iota(jnp.int32, (WKV_PW, WKV_PW), 0)
    cols = lax.broadcasted_iota(jnp.int32, (WKV_PW, WKV_PW), 1)
    head_shift = RWKV_HEAD.bit_length() - 1
    same_head = (rows >> head_shift) == (cols >> head_shift)
    return same_head, rows & (RWKV_HEAD - 1), cols & (RWKV_HEAD - 1), rows == cols


def _wkv_intra_kernel(r_ref, v_ref, kk_ref, lwf_ref, kdf_ref, bf_ref, lwb_ref, kdb_ref, bb_ref,
                      qf_ref, y0f_ref, pf_ref, hf_ref, gf_ref, qb_ref, y0b_ref, pb_ref, hb_ref, gb_ref):
    C, PW, NP = WKV_CHUNK, WKV_PW, WKV_PAIR
    same_head, t_i, s_i, diag = _same_head_mask()
    eye = diag.astype(F32)
    t_c = lax.broadcasted_iota(jnp.int32, (C, C), 0)
    s_c = lax.broadcasted_iota(jnp.int32, (C, C), 1)

    def bd(x):
        return jnp.where(same_head, _tile_rows(x, NP), jnp.zeros((), x.dtype))

    def one_chunk(ci, carry):
        rows = pl.ds(pl.multiple_of(ci * C, C), C)
        r, v, kk = r_ref[rows, :].astype(F32), v_ref[rows, :], kk_ref[rows, :].astype(F32)
        probs = []
        for reverse, (lw_ref, kd_ref, b_ref), out_refs in (
                (False, (lwf_ref, kdf_ref, bf_ref), (qf_ref, y0f_ref, pf_ref, hf_ref, gf_ref)),
                (True, (lwb_ref, kdb_ref, bb_ref), (qb_ref, y0b_ref, pb_ref, hb_ref, gb_ref))):
            cum_mat = ((s_c >= t_c) if reverse else (s_c <= t_c)).astype(BF16)
            earlier = same_head & ((s_i > t_i) if reverse else (s_i < t_i))
            readable = earlier if reverse else (same_head & (s_i <= t_i))
            last_row = 0 if reverse else C - 1

            lw = lw_ref[rows, :]
            lw_hi, lw_lo = _split_bf16(lw)
            cum = (jnp.dot(cum_mat, lw_hi, preferred_element_type=F32)
                   + jnp.dot(cum_mat, lw_lo, preferred_element_type=F32))
            g_ex = jnp.exp(cum - lw)
            g_inv = jnp.exp(-cum)
            g_end = jnp.exp(cum[last_row:last_row + 1, :])
            al_all = -kk * g_ex
            be_all = b_ref[rows, :].astype(F32) * g_inv
            kb_all = kd_ref[rows, :].astype(F32) * g_inv
            rb_all = r * (g_ex if reverse else jnp.exp(cum))
            bt_all = be_all * g_end
            kt_all = kb_all * g_end
            out_refs[4][ci] = g_end
            for p in range(D_RWKV // PW):
                sl = slice(p * PW, (p + 1) * PW)
                probs.append(dict(sl=sl, out=out_refs, earlier=earlier, readable=readable,
                                  al=al_all[:, sl], be=be_all[:, sl], kb=kb_all[:, sl],
                                  rb=rb_all[:, sl], v=v[:, sl], bt=bt_all[:, sl], kt=kt_all[:, sl]))
        _wkv_intra_solve(probs, bd, eye, same_head, rows)
        return carry

    lax.fori_loop(0, WKV_INTRA_CHUNKS, one_chunk, 0)


def _wkv_intra_solve(probs, bd, eye, same_head, rows):
    C, PW, NP = WKV_CHUNK, WKV_PW, WKV_PAIR
    for c in probs:
        c["al_bd"] = bd(c["al"])
        aa = _dot_nt(jnp.concatenate([c["al_bd"], bd(c["rb"])], axis=0),
                     jnp.concatenate([_tile_rows(c["be"], NP), _tile_rows(c["kb"], NP)], axis=0))
        c["a_ab"] = jnp.where(c["earlier"], aa[:PW, :PW], 0.0)
        a_ak = jnp.where(c["earlier"], aa[:PW, PW:], 0.0)
        c["a_rb"] = jnp.where(c["readable"], aa[PW:, :PW], 0.0)
        a_rk = jnp.where(c["readable"], aa[PW:, PW:], 0.0)
        c["a_kk"] = jnp.concatenate([a_ak, a_rk], axis=0)
    for c in probs:
        c["akv"] = _dot(c["a_kk"], bd(c["v"]))
    for c in probs:
        c["inv"] = eye + c["a_ab"]
        c["pk"] = _dot(c["a_ab"], c["a_ab"])
    for _ in range(C.bit_length() - 3):
        for c in probs:
            sq_and_apply = _dot(c["pk"], jnp.concatenate([c["pk"], c["inv"]], axis=1))
            c["inv"] = c["inv"] + sq_and_apply[:, PW:]
            c["pk"] = sq_and_apply[:, :PW]
    for c in probs:
        c["inv"] = c["inv"] + _dot(c["pk"], c["inv"])
    for c in probs:
        c["wu"] = _dot(c["inv"], jnp.concatenate([c["al_bd"], c["akv"][:PW]], axis=1))
    for c in probs:
        c["qy"] = _dot(c["a_rb"], c["wu"])
    for c in probs:
        q_ref, y0_ref, p_ref, h_ref, _ = c["out"]
        sl, bt = c["sl"], c["bt"]
        q_ref[rows, sl] = (c["rb"] + _fold_rows(c["qy"][:, :PW], NP)).astype(BF16)
        y0_ref[rows, sl] = _fold_rows(c["qy"][:, PW:] + c["akv"][PW:], NP)
        w = _fold_rows(c["wu"][:, :PW], NP)
        u0 = _fold_rows(c["wu"][:, PW:], NP)
        pm = _dot_tn(w, bt)
        hm = _dot_tn(jnp.concatenate([u0.astype(BF16), c["v"]], axis=0),
                     jnp.concatenate([bt, c["kt"]], axis=0))
        p_ref[rows, sl] = _fold_rows(jnp.where(same_head, pm, 0.0), NP).astype(BF16)
        h_ref[rows, sl] = _fold_rows(jnp.where(same_head, hm, 0.0), NP)


def _wkv_intra(r, v, kk, lwf, kdf, bf, lwb, kdb, bb):
    m = r.shape[0]
    nb = WKV_INTRA_CHUNKS
    n_chunks = m // WKV_CHUNK
    spec = pl.BlockSpec((nb * WKV_CHUNK, D_RWKV), lambda i: (i, 0))
    gspec = pl.BlockSpec((nb, 1, D_RWKV), lambda i: (i, 0, 0))
    f32_tok = jax.ShapeDtypeStruct((m, D_RWKV), F32)
    bf_tok = jax.ShapeDtypeStruct((m, D_RWKV), BF16)
    g_shape = jax.ShapeDtypeStruct((n_chunks, 1, D_RWKV), F32)
    return pl.pallas_call(
        _wkv_intra_kernel, grid=(n_chunks // nb,),
        in_specs=[spec] * 9,
        out_specs=[spec, spec, spec, spec, gspec] * 2,
        out_shape=[bf_tok, f32_tok, bf_tok, f32_tok, g_shape] * 2,
        compiler_params=_cparams(), name="wkv_intra")(r, v, kk, lwf, kdf, bf, lwb, kdb, bb)


def _wkv_inter_kernel(seq, qf_ref, y0f_ref, pf_ref, hf_ref, gf_ref, qb_ref, y0b_ref, pb_ref, hb_ref,
                      gb_ref, yf_ref, yb_ref, s_ref):
    C, PW, NP, NB = WKV_CHUNK, WKV_PW, WKV_PAIR, WKV_INTER_CHUNKS
    n_steps = pl.num_programs(0)
    i = pl.program_id(0)
    same_head = _same_head_mask()[0]

    @pl.when(i == 0)
    def _():
        s_ref[...] = jnp.zeros_like(s_ref)

    dirs = ((qf_ref, y0f_ref, pf_ref, hf_ref, gf_ref, yf_ref),
            (qb_ref, y0b_ref, pb_ref, hb_ref, gb_ref, yb_ref))
    for jj in range(NB):
        chains = []
        for d, (q_ref, y0_ref, p_ref, h_ref, g_ref, y_ref) in enumerate(dirs):
            reverse = d == 1
            blk = (n_steps - 1 - i) if reverse else i
            j = NB - 1 - jj if reverse else jj
            chunk = blk * NB + j
            fresh = _seq_start((chunk + 1) * C if reverse else chunk * C, *seq)
            rs = slice(j * C, (j + 1) * C)
            g = g_ref[j]
            for p in range(D_RWKV // PW):
                sl = slice(p * PW, (p + 1) * PW)
                s0 = jnp.where(fresh, 0.0, s_ref[d, p])
                pm = jnp.where(same_head, _tile_rows(p_ref[rs, sl], NP), 0.0)
                chains.append((d, p, rs, sl, s0, _dot(s0, pm), g[:, sl]))
        for d, p, rs, sl, s0, s0_p, g in chains:
            q_ref, y0_ref, _, h_ref, _, y_ref = dirs[d]
            y_ref[rs, sl] = _dot_nt(q_ref[rs, sl], s0) + y0_ref[rs, sl]
            hm = jnp.where(same_head, _tile_rows(h_ref[rs, sl], NP), 0.0)
            s_ref[d, p] = s0 * g + s0_p + hm


def _wkv_inter(qf, y0f, pf, hf, gf, qb, y0b, pb, hb, gb, seq):
    m = qf.shape[0]
    nb = WKV_INTER_CHUNKS
    tb = nb * WKV_CHUNK
    n_steps = m // tb
    fwd = pl.BlockSpec((tb, D_RWKV), lambda i: (i, 0))
    bwd = pl.BlockSpec((tb, D_RWKV), lambda i: (n_steps - 1 - i, 0))
    gfwd = pl.BlockSpec((nb, 1, D_RWKV), lambda i: (i, 0, 0))
    gbwd = pl.BlockSpec((nb, 1, D_RWKV), lambda i: (n_steps - 1 - i, 0, 0))
    out = jax.ShapeDtypeStruct((m, D_RWKV), F32)
    return pl.pallas_call(
        functools.partial(_wkv_inter_kernel, seq), grid=(n_steps,),
        in_specs=[fwd] * 4 + [gfwd] + [bwd] * 4 + [gbwd],
        out_specs=[fwd, bwd], out_shape=[out, out],
        scratch_shapes=[pltpu.VMEM((2, D_RWKV // WKV_PW, WKV_PW, WKV_PW), F32)],
        compiler_params=_cparams(), name="wkv_inter")(qf, y0f, pf, hf, gf, qb, y0b, pb, hb, gb)


def _post_kernel(x_ref, yf_ref, yb_ref, bonus_ref, g_ref, conv_ref, gt_ref, hs_ref,
                 lnxg_ref, lnxb_ref, wrb_ref, wout_ref, bout_ref, ln1g_ref, ln1b_ref,
                 wrh_ref, wrl_ref, br_ref,
                 x1_ref, x1b_ref, tope_ref, gate_ref):
    hs = hs_ref[...]
    y = yf_ref[...] + yb_ref[...]
    inv_n = 1.0 / RWKV_HEAD
    mu = _dot_hilo(y, hs) * inv_n
    yc = y - mu
    var = _dot_hilo(yc * yc, hs) * inv_n
    y = yc * lax.rsqrt(var + GN_EPS) * lnxg_ref[...] + lnxb_ref[...]
    y = (y + bonus_ref[...].astype(F32)) * g_ref[...].astype(F32)
    rwkv_out = _dot(y, wrb_ref[...])
    gt = gt_ref[...]
    mixed = (jax.nn.sigmoid(gt[:, :D_MODEL]) * conv_ref[...]
             + jax.nn.sigmoid(gt[:, D_MODEL:]) * rwkv_out)
    mix = _dot(mixed, wout_ref[...]) + bout_ref[...]
    x1 = _layer_norm(DEEPNORM_ALPHA * x_ref[...] + mix, ln1g_ref[...], ln1b_ref[...], LN_EPS)
    x1_ref[...] = x1
    x1b_ref[...] = x1.astype(BF16)

    xh, xl = _split_bf16(x1)
    logits = (jnp.dot(xh, wrh_ref[...], preferred_element_type=F32)
              + jnp.dot(xh, wrl_ref[...], preferred_element_type=F32)
              + jnp.dot(xl, wrh_ref[...], preferred_element_type=F32)) + br_ref[...]
    lane = lax.broadcasted_iota(jnp.int32, logits.shape, 1).astype(F32)
    neg = jnp.float32(-jnp.inf)
    logits = jnp.where(lane < N_EXPERTS, logits, neg)
    tops, idxs = [], []
    for _ in range(TOP_K):
        mx = jnp.max(logits, axis=-1, keepdims=True)
        ix = jnp.min(jnp.where(logits == mx, lane, float(LANES)), axis=-1, keepdims=True)
        tops.append(mx)
        idxs.append(ix)
        logits = jnp.where(lane == ix, neg, logits)
    es = [jnp.exp(t - tops[0]) for t in tops]
    denom = es[0] + es[1] + es[2] + es[3]
    tope = jnp.zeros(logits.shape, F32)
    gate = jnp.zeros(logits.shape, F32)
    for j in range(TOP_K):
        tope = jnp.where(lane == j, idxs[j], tope)
        gate = jnp.where(lane == j, es[j] / denom, gate)
    tope_ref[...] = tope.astype(jnp.int32)
    gate_ref[...] = gate


def _post(x, yf, yb, bonus, g, conv, gt, hs, lnxg, lnxb, wrb, wout, bout, ln1g, ln1b,
          wrh, wrl, br, layer):
    m = x.shape[0]
    tm = TOK_TILE
    return pl.pallas_call(
        _post_kernel, grid=(m // tm,),
        in_specs=[_row_spec(tm, D_MODEL)] + [_row_spec(tm, D_RWKV)] * 4
                 + [_row_spec(tm, D_MODEL), _row_spec(tm, N_GATES), _const_spec((D_RWKV, D_RWKV)),
                    _const_spec((1, D_RWKV), layer), _const_spec((1, D_RWKV), layer),
                    _const_spec((D_RWKV, D_MODEL), layer), _const_spec((D_MODEL, D_MODEL), layer),
                    _const_spec((1, D_MODEL), layer), _const_spec((1, D_MODEL), layer),
                    _const_spec((1, D_MODEL), layer),
                    _const_spec((D_MODEL, LANES), layer), _const_spec((D_MODEL, LANES), layer),
                    _const_spec((1, LANES), layer)],
        out_specs=[_row_spec(tm, D_MODEL), _row_spec(tm, D_MODEL), _row_spec(tm, LANES),
                   _row_spec(tm, LANES)],
        out_shape=[jax.ShapeDtypeStruct((m, D_MODEL), F32), jax.ShapeDtypeStruct((m, D_MODEL), BF16),
                   jax.ShapeDtypeStruct((m, LANES), jnp.int32), jax.ShapeDtypeStruct((m, LANES), F32)],
        compiler_params=_cparams(), name="post")(
            x, yf, yb, bonus, g, conv, gt, hs, lnxg, lnxb, wrb, wout, bout, ln1g, ln1b, wrh, wrl, br)


def _moe_kernel(blk_e_ref, nvalid_ref, x_ref, wgu_ref, bgu_ref, wd_ref, bd_ref, o_ref):
    i = pl.program_id(0)

    @pl.when(i < nvalid_ref[0])
    def _():
        hgu = jnp.dot(x_ref[...], wgu_ref[...], preferred_element_type=F32) + bgu_ref[...]
        hg = jnp.minimum(hgu[:, :D_EXPERT], SWIGLU_LIMIT)
        hl = jnp.clip(hgu[:, D_EXPERT:], -SWIGLU_LIMIT, SWIGLU_LIMIT)
        act = hg * jax.nn.sigmoid(SWIGLU_ALPHA * hg) * (hl + 1.0)
        o_ref[...] = (_dot(act, wd_ref[...]) + bd_ref[...]).astype(o_ref.dtype)

    @pl.when(i >= nvalid_ref[0])
    def _():
        o_ref[...] = jnp.zeros_like(o_ref)


def _moe(blk_e, nvalid, xs, wgu, bgu, wd, bd, layer):
    n_rows = xs.shape[0]
    tm = MOE_TILE
    grid_spec = pltpu.PrefetchScalarGridSpec(
        num_scalar_prefetch=2, grid=(n_rows // tm,),
        in_specs=[pl.BlockSpec((tm, D_MODEL), lambda i, be, nv: (i, 0)),
                  pl.BlockSpec((None, None, D_MODEL, 2 * D_EXPERT), lambda i, be, nv: (layer, be[i], 0, 0)),
                  pl.BlockSpec((None, None, 1, 2 * D_EXPERT), lambda i, be, nv: (layer, be[i], 0, 0)),
                  pl.BlockSpec((None, None, D_EXPERT, D_MODEL), lambda i, be, nv: (layer, be[i], 0, 0)),
                  pl.BlockSpec((None, None, 1, D_MODEL), lambda i, be, nv: (layer, be[i], 0, 0))],
        out_specs=pl.BlockSpec((tm, D_MODEL), lambda i, be, nv: (i, 0)))
    return pl.pallas_call(
        _moe_kernel, grid_spec=grid_spec,
        out_shape=jax.ShapeDtypeStruct((n_rows, D_MODEL), BF16),
        compiler_params=_cparams(), name="moe")(blk_e, nvalid, xs, wgu, bgu, wd, bd)


def _route(top_e, tile):
    m = top_e.shape[0]
    n_assign = m * TOP_K
    flat_e = top_e.reshape(-1)
    iota = jnp.arange(n_assign, dtype=jnp.int32)
    experts = jnp.arange(N_EXPERTS, dtype=jnp.int32)
    _, order = lax.sort((flat_e, iota), num_keys=1, is_stable=True)
    _, rank = lax.sort((order, iota), num_keys=1)
    is_e = flat_e[:, None] == experts[None, :]
    counts = jnp.sum(is_e, axis=0, dtype=jnp.int32)
    start = jnp.cumsum(counts) - counts
    padded = (counts + tile - 1) // tile * tile
    pend = jnp.cumsum(padded)
    shift = pend - padded - start
    pos = rank + jnp.sum(jnp.where(is_e, shift[None, :], 0), axis=1)
    n_blocks = n_assign // tile + N_EXPERTS
    blk_first = jnp.arange(n_blocks, dtype=jnp.int32) * tile
    blk_e = jnp.minimum(jnp.sum(pend[None, :] <= blk_first[:, None], axis=1, dtype=jnp.int32),
                        N_EXPERTS - 1)
    src = (blk_first - shift[blk_e])[:, None] + jnp.arange(tile, dtype=jnp.int32)[None, :]
    valid = src < (start + counts)[blk_e][:, None]
    row_tok = jnp.where(valid, order[jnp.clip(src, 0, n_assign - 1)] // TOP_K, 0).reshape(-1)
    nvalid = (pend[-1:] // tile).astype(jnp.int32)
    return row_tok, pos.reshape(m, TOP_K), blk_e, nvalid


def _final_kernel(n_first, x1_ref, yga_ref, ygb_ref, gate_ref, p_ref, wpg_ref, bpg_ref, wpp_ref,
                  g_ref, b_ref, o_ref):
    x1 = x1_ref[...]
    gate = gate_ref[...]
    in_first = pl.program_id(0) < n_first
    ffn = jnp.zeros(x1.shape, F32)
    for j in range(TOP_K):
        yj = jnp.where(in_first, yga_ref[j], ygb_ref[j]).astype(F32)
        ffn = ffn + gate[:, j:j + 1] * yj
    ple = jax.nn.sigmoid(_dot(x1, wpg_ref[...]) + bpg_ref[...]) * _dot(p_ref[...], wpp_ref[...])
    o_ref[...] = _layer_norm(DEEPNORM_ALPHA * x1 + ffn + ple, g_ref[...], b_ref[...], LN_EPS)


def _final(x1, yg_a, yg_b, gate, p, wpg, bpg, wpp, g, b, layer):
    m = x1.shape[0]
    tm = TOK_TILE
    n_first = yg_a.shape[1] // tm
    return pl.pallas_call(
        functools.partial(_final_kernel, n_first), grid=(m // tm,),
        in_specs=[_row_spec(tm, D_MODEL),
                  pl.BlockSpec((TOP_K, tm, D_MODEL), lambda i: (0, jnp.minimum(i, n_first - 1), 0)),
                  pl.BlockSpec((TOP_K, tm, D_MODEL), lambda i: (0, jnp.maximum(i - n_first, 0), 0)),
                  _row_spec(tm, LANES),
                  pl.BlockSpec((None, tm, D_PLE), lambda i: (layer, i, 0)),
                  _const_spec((D_MODEL, D_MODEL), layer), _const_spec((1, D_MODEL), layer),
                  _const_spec((D_PLE, D_MODEL), layer), _const_spec((1, D_MODEL), layer),
                  _const_spec((1, D_MODEL), layer)],
        out_specs=_row_spec(tm, D_MODEL),
        out_shape=jax.ShapeDtypeStruct((m, D_MODEL), F32),
        compiler_params=_cparams(), name="final")(x1, yg_a, yg_b, gate, p, wpg, bpg, wpp, g, b)


def _pad_lowrank(w):
    z = jnp.zeros_like(w[:, 0])
    fwd = jnp.concatenate([w[:, 0], z], axis=1)
    bwd = jnp.concatenate([z, w[:, 1]], axis=1)
    return jnp.stack([fwd, bwd], axis=1)


def kernel(x_prompt, x_sample, p_prompt, p_sample, ln0_g, ln0_b, w_in, b_in, mu_shift, dw_w, dw_b, cln_g, cln_b, w_cb, b_cb, w0, w2, a0, a2, g2, k_k, k_a, r_k, lnx_g, lnx_b, w_rb, w_out, b_out, ln1_g, ln1_b, w_router, b_router, w_gu, b_gu, w_down, b_down, w_pg, b_pg, w_pp, ln2_g, ln2_b):
    bp, tp, d = x_prompt.shape
    bs, ts, _ = x_sample.shape
    mp, ms = bp * tp, bs * ts
    m = mp + ms
    assert d == D_MODEL and tp % TOK_TILE == 0 and ts % TOK_TILE == 0
    assert m % LN0_TILE == 0 and (m // 2) % TOK_TILE == 0 and (m // 2 * TOP_K) % MOE_TILE == 0
    assert m % (WKV_INTRA_CHUNKS * WKV_CHUNK) == 0 and m % (WKV_INTER_CHUNKS * WKV_CHUNK) == 0
    seq = (mp, tp, ts, m)
    L = w_in.shape[0]

    x = jnp.concatenate([x_prompt.reshape(mp, d), x_sample.reshape(ms, d)], axis=0)
    p = jnp.concatenate([p_prompt.reshape(L, mp, D_PLE), p_sample.reshape(L, ms, D_PLE)], axis=1)

    row = lambda a: a.reshape(a.shape[0], 1, -1)
    wc = w_in[:, :, :N_GLU].astype(BF16)
    wz = w_in[:, :, N_GLU:N_GLU + N_SHIFT].astype(BF16)
    wg = w_in[:, :, N_GLU + N_SHIFT:].astype(BF16)
    bc, bz, bg = row(b_in[:, :N_GLU]), row(b_in[:, N_GLU:N_GLU + N_SHIFT]), row(b_in[:, N_GLU + N_SHIFT:])
    w2p = _pad_lowrank(w2).astype(BF16)
    a2p = _pad_lowrank(a2).astype(BF16)
    head = jnp.arange(D_RWKV, dtype=jnp.int32) // RWKV_HEAD
    hs = (head[:, None] == head[None, :]).astype(BF16)
    wr_pad = jnp.pad(w_router, ((0, 0), (0, 0), (0, LANES - N_EXPERTS)))
    wrh = wr_pad.astype(BF16)
    wrl = (wr_pad - wrh.astype(F32)).astype(BF16)
    br = row(jnp.pad(b_router, ((0, 0), (0, LANES - N_EXPERTS))))
    wgu_b = w_gu.astype(BF16)
    wd_b = w_down.astype(BF16)
    bgu = b_gu.reshape(L, N_EXPERTS, 1, 2 * D_EXPERT)
    bdn = b_down.reshape(L, N_EXPERTS, 1, D_MODEL)
    wcb_b, wrb_b, wout_b = w_cb.astype(BF16), w_rb.astype(BF16), w_out.astype(BF16)
    wpg_b, wpp_b, g2_b = w_pg.astype(BF16), w_pp.astype(BF16), g2.astype(BF16)

    x = _ln0(x, ln0_g, ln0_b)
    for l in range(L):
        c, z, gt = _in_proj(x, wc, wz, wg, bc, bz, bg, l)
        conv = _conv_branch(c, dw_w, row(dw_b), row(cln_g), row(cln_b), wcb_b, row(b_cb), l, seq)
        (r, v, kk, g, bonus, lwf, kdf, bf, lwb, kdb, bb) = _rwkv_prep(
            z, mu_shift, w0, w2p, a0, a2p, g2_b, row(k_k), row(k_a), row(r_k.reshape(L, D_RWKV)), hs, l, seq)
        yf, yb = _wkv_inter(*_wkv_intra(r, v, kk, lwf, kdf, bf, lwb, kdb, bb), seq)
        x1, x1b, tope, gate = _post(
            x, yf, yb, bonus, g, conv, gt, hs, row(lnx_g), row(lnx_b), wrb_b, wout_b, row(b_out),
            row(ln1_g), row(ln1_b), wrh, wrl, br, l)
        ygs = []
        for h in range(2):
            lo = h * (m // 2)
            row_tok, pos, blk_e, nvalid = _route(tope[lo:lo + m // 2, :TOP_K], MOE_TILE)
            y_rows = _moe(blk_e, nvalid, x1b[row_tok + lo], wgu_b, bgu, wd_b, bdn, l)
            ygs.append(y_rows[pos.T])
        x = _final(x1, ygs[0], ygs[1], gate, p, wpg_b, row(b_pg), wpp_b, row(ln2_g), row(ln2_b), l)

    return (x[:mp].reshape(bp, tp, d), x[mp:].reshape(bs, ts, d))
```

```python
import functools
import math

import jax
import jax.numpy as jnp
from jax import lax
from jax.experimental import pallas as pl
from jax.experimental.pallas import tpu as pltpu

F32 = jnp.float32
BF16 = jnp.bfloat16

D_MODEL = 1024
DEPTH = 4
D_CONV = 512
CONV_WIDTH = 31
CONV_HALO = 16
D_RWKV = 512
RWKV_HEAD = 64
DECAY_RANK = 64
ICLR_RANK = 64
GATE_RANK = 128
N_EXPERTS = 32
TOP_K = 4
D_EXPERT = 1024
SWIGLU_LIMIT = 7.0
SWIGLU_ALPHA = 1.702
D_PLE = 256
LN_EPS = 1e-5
GN_EPS = 64e-5
DEEPNORM_ALPHA = (2 * DEPTH) ** 0.25
DECAY_SCALE = math.exp(-0.5)
N_GLU = 2 * D_CONV
N_SHIFT = 3 * D_RWKV + 2 * DECAY_RANK + 2 * ICLR_RANK + GATE_RANK
N_GATES = 2 * D_MODEL
N_LOWRANK = 2 * DECAY_RANK

LANES = 128
SUBLANES = 8
SHIFT_HALO = SUBLANES
TOK_TILE = 256
LN0_TILE = 512
MOE_TILE = 512
MOE_GROUPS = 2
WKV_CHUNK = 64
WKV_PAIR = 2
WKV_PW = WKV_PAIR * RWKV_HEAD
WKV_INTER_CHUNKS = 4
WKV_INTRA_CHUNKS = 4
VMEM_LIMIT = 48 * 1024 * 1024
MOE_VMEM_LIMIT = 58 * 1024 * 1024
assert WKV_CHUNK == RWKV_HEAD


def _cparams(n_axes=1):
    return pltpu.CompilerParams(dimension_semantics=("arbitrary",) * n_axes,
                                vmem_limit_bytes=VMEM_LIMIT)


def _dot(a, b):
    return jnp.dot(a.astype(BF16), b.astype(BF16), preferred_element_type=F32)


def _dot_nt(a, b):
    return lax.dot_general(a.astype(BF16), b.astype(BF16), (((1,), (1,)), ((), ())),
                           preferred_element_type=F32)


def _dot_tn(a, b):
    return lax.dot_general(a.astype(BF16), b.astype(BF16), (((0,), (0,)), ((), ())),
                           preferred_element_type=F32)


def _split_bf16(x):
    hi = x.astype(BF16)
    lo = (x - hi.astype(F32)).astype(BF16)
    return hi, lo


def _layer_norm(x, g, b, eps):
    mu = jnp.mean(x, axis=-1, keepdims=True)
    xc = x - mu
    var = jnp.mean(xc * xc, axis=-1, keepdims=True)
    return xc * lax.rsqrt(var + eps) * g + b


def _seq_start(tok, m_prompt, t_prompt, t_sample, m_total):
    in_prompt = tok < m_prompt
    at_prompt_start = lax.rem(tok, jnp.int32(t_prompt)) == 0
    at_sample_start = lax.rem(jnp.abs(tok - m_prompt), jnp.int32(t_sample)) == 0
    return ((in_prompt & at_prompt_start) | (jnp.logical_not(in_prompt) & at_sample_start)
            | (tok >= m_total))


def _const_spec(shape, layer=None):
    if layer is None:
        nd = len(shape)
        return pl.BlockSpec(shape, lambda i, _n=nd: (0,) * _n)
    nd = len(shape)
    return pl.BlockSpec((None,) + tuple(shape), lambda i, _l=layer, _n=nd: (_l,) + (0,) * _n)


def _row_spec(tm, width):
    return pl.BlockSpec((tm, width), lambda i: (i, 0))


def _ln_kernel(x_ref, g_ref, b_ref, o_ref):
    o_ref[...] = _layer_norm(x_ref[...], g_ref[...], b_ref[...], LN_EPS)


def _ln0(x, g, b):
    m = x.shape[0]
    tm = LN0_TILE
    return pl.pallas_call(
        _ln_kernel, grid=(m // tm,),
        in_specs=[_row_spec(tm, D_MODEL), _const_spec((1, D_MODEL)), _const_spec((1, D_MODEL))],
        out_specs=_row_spec(tm, D_MODEL),
        out_shape=jax.ShapeDtypeStruct((m, D_MODEL), F32),
        compiler_params=_cparams(), name="ln0")(x, g.reshape(1, -1), b.reshape(1, -1))


def _in_proj_kernel(x_ref, wc_ref, wz_ref, wg_ref, bc_ref, bz_ref, bg_ref, c_ref, z_ref, gt_ref):
    xb = x_ref[...].astype(BF16)
    c_ref[...] = jnp.dot(xb, wc_ref[...], preferred_element_type=F32) + bc_ref[...]
    z_ref[...] = jnp.dot(xb, wz_ref[...], preferred_element_type=F32) + bz_ref[...]
    gt_ref[...] = jnp.dot(xb, wg_ref[...], preferred_element_type=F32) + bg_ref[...]


def _in_proj(x, wc, wz, wg, bc, bz, bg, layer):
    m = x.shape[0]
    tm = TOK_TILE
    return pl.pallas_call(
        _in_proj_kernel, grid=(m // tm,),
        in_specs=[_row_spec(tm, D_MODEL),
                  _const_spec((D_MODEL, N_GLU), layer), _const_spec((D_MODEL, N_SHIFT), layer),
                  _const_spec((D_MODEL, N_GATES), layer),
                  _const_spec((1, N_GLU), layer), _const_spec((1, N_SHIFT), layer),
                  _const_spec((1, N_GATES), layer)],
        out_specs=[_row_spec(tm, N_GLU), _row_spec(tm, N_SHIFT), _row_spec(tm, N_GATES)],
        out_shape=[jax.ShapeDtypeStruct((m, N_GLU), F32), jax.ShapeDtypeStruct((m, N_SHIFT), F32),
                   jax.ShapeDtypeStruct((m, N_GATES), F32)],
        compiler_params=_cparams(), name="in_proj")(x, wc, wz, wg, bc, bz, bg)


def _conv_kernel(seq, c_ref, cp_ref, cn_ref, dww_ref, dwb_ref, g_ref, b_ref, wcb_ref, bcb_ref,
                 o_ref, u_ref, sh_ref):
    tm = c_ref.shape[0]
    i = pl.program_id(0)
    first = _seq_start(i * tm, *seq)
    last = _seq_start((i + 1) * tm, *seq)

    def glu(c):
        return c[:, :D_CONV] * jax.nn.sigmoid(c[:, D_CONV:])

    u_ref[0:CONV_HALO, :] = jnp.where(first, 0.0, glu(cp_ref[...]))
    u_ref[CONV_HALO:CONV_HALO + tm, :] = glu(c_ref[...])
    u_ref[CONV_HALO + tm:, :] = jnp.where(last, 0.0, glu(cn_ref[...]))

    span = sh_ref.shape[1]
    for s in range(SUBLANES):
        sh_ref[s] = u_ref[s:s + span, :]
    base = CONV_HALO - CONV_WIDTH // 2
    acc = jnp.zeros((tm, D_CONV), F32) + dwb_ref[...]
    for j in range(CONV_WIDTH):
        a, s = divmod(base + j, SUBLANES)
        acc = acc + sh_ref[s, a * SUBLANES:a * SUBLANES + tm, :] * dww_ref[j:j + 1, :]
    u = _layer_norm(acc, g_ref[...], b_ref[...], LN_EPS)
    u = u * jax.nn.sigmoid(u)
    o_ref[...] = _dot(u, wcb_ref[...]) + bcb_ref[...]


def _conv_branch(c, dww, dwb, g, b, wcb, bcb, layer, seq):
    m = c.shape[0]
    tm = TOK_TILE
    r = tm // CONV_HALO
    nh = m // CONV_HALO
    return pl.pallas_call(
        functools.partial(_conv_kernel, seq), grid=(m // tm,),
        in_specs=[_row_spec(tm, N_GLU),
                  pl.BlockSpec((CONV_HALO, N_GLU), lambda i: (jnp.maximum(i * r - 1, 0), 0)),
                  pl.BlockSpec((CONV_HALO, N_GLU), lambda i: (jnp.minimum((i + 1) * r, nh - 1), 0)),
                  _const_spec((CONV_WIDTH, D_CONV), layer), _const_spec((1, D_CONV), layer),
                  _const_spec((1, D_CONV), layer), _const_spec((1, D_CONV), layer),
                  _const_spec((D_CONV, D_MODEL), layer), _const_spec((1, D_MODEL), layer)],
        out_specs=_row_spec(tm, D_MODEL),
        out_shape=jax.ShapeDtypeStruct((m, D_MODEL), F32),
        scratch_shapes=[pltpu.VMEM((tm + 2 * CONV_HALO, D_CONV), F32),
                        pltpu.VMEM((SUBLANES, tm + 2 * CONV_HALO - SUBLANES, D_CONV), F32)],
        compiler_params=_cparams(), name="conv_branch")(c, c, c, dww, dwb, g, b, wcb, bcb)


def _rwkv_prep_kernel(seq, z_ref, zp_ref, zn_ref, mu_ref, w0_ref, w2_ref, a0_ref, a2_ref, g2_ref,
                      kk_ref, ka_ref, rk_ref, hs_ref,
                      r_ref, v_ref, kkn_ref, g_ref, bonus_ref,
                      lwf_ref, kdf_ref, bf_ref, lwb_ref, kdb_ref, bb_ref):
    tm = z_ref.shape[0]
    i = pl.program_id(0)
    first = _seq_start(i * tm, *seq)
    last = _seq_start((i + 1) * tm, *seq)

    z = z_ref[...]
    row = lax.broadcasted_iota(jnp.int32, (tm, 1), 0)
    prev_row = jnp.where(first, 0.0, zp_ref[SHIFT_HALO - 1:SHIFT_HALO, :])
    next_row = jnp.where(last, 0.0, zn_ref[0:1, :])
    zp = jnp.where(row == 0, prev_row, pltpu.roll(z, 1, 0))
    zn = jnp.where(row == tm - 1, next_row, pltpu.roll(z, tm - 1, 0))
    zs = z + mu_ref[0:1, :] * (zp - z) + mu_ref[1:2, :] * (zn - z)

    o = 3 * D_RWKV
    r = zs[:, 0:D_RWKV]
    k = zs[:, D_RWKV:2 * D_RWKV]
    v = zs[:, 2 * D_RWKV:o]
    lw = zs[:, o:o + N_LOWRANK]
    la = zs[:, o + N_LOWRANK:o + 2 * N_LOWRANK]
    lg = zs[:, o + 2 * N_LOWRANK:]

    hs = hs_ref[...]
    kk = k * kk_ref[...]
    kk = kk * lax.rsqrt(_dot(kk * kk, hs) + 1e-12)
    r_ref[...] = r.astype(BF16)
    v_ref[...] = v.astype(BF16)
    kkn_ref[...] = kk.astype(BF16)
    g_ref[...] = _dot(jax.nn.sigmoid(lg), g2_ref[...]).astype(BF16)
    bonus_ref[...] = (_dot(r * k * rk_ref[...], hs) * v).astype(BF16)

    tlw = jnp.tanh(lw)
    ka = ka_ref[...]
    for d, (lw_o, kd_o, b_o) in enumerate(((lwf_ref, kdf_ref, bf_ref), (lwb_ref, kdb_ref, bb_ref))):
        t = w0_ref[d:d + 1, :] + _dot(tlw, w2_ref[d])
        a = jax.nn.sigmoid(a0_ref[d:d + 1, :] + _dot(la, a2_ref[d]))
        lw_o[...] = -DECAY_SCALE * jax.nn.sigmoid(t)
        kd_o[...] = (k * (1.0 + (a - 1.0) * ka)).astype(BF16)
        b_o[...] = (kk * a).astype(BF16)


def _rwkv_prep(z, mu, w0, w2p, a0, a2p, g2, k_k, k_a, r_k, hs, layer, seq):
    m = z.shape[0]
    tm = TOK_TILE
    r = tm // SHIFT_HALO
    nh = m // SHIFT_HALO
    f32_tok = jax.ShapeDtypeStruct((m, D_RWKV), F32)
    bf_tok = jax.ShapeDtypeStruct((m, D_RWKV), BF16)
    outs = [bf_tok] * 5 + [f32_tok, bf_tok, bf_tok] * 2
    return pl.pallas_call(
        functools.partial(_rwkv_prep_kernel, seq), grid=(m // tm,),
        in_specs=[_row_spec(tm, N_SHIFT),
                  pl.BlockSpec((SHIFT_HALO, N_SHIFT), lambda i: (jnp.maximum(i * r - 1, 0), 0)),
                  pl.BlockSpec((SHIFT_HALO, N_SHIFT), lambda i: (jnp.minimum((i + 1) * r, nh - 1), 0)),
                  _const_spec((2, N_SHIFT), layer), _const_spec((2, D_RWKV), layer),
                  _const_spec((2, N_LOWRANK, D_RWKV), layer), _const_spec((2, D_RWKV), layer),
                  _const_spec((2, N_LOWRANK, D_RWKV), layer), _const_spec((GATE_RANK, D_RWKV), layer),
                  _const_spec((1, D_RWKV), layer), _const_spec((1, D_RWKV), layer),
                  _const_spec((1, D_RWKV), layer), _const_spec((D_RWKV, D_RWKV))],
        out_specs=[_row_spec(tm, D_RWKV)] * 11,
        out_shape=outs,
        compiler_params=_cparams(), name="rwkv_prep")(z, z, z, mu, w0, w2p, a0, a2p, g2, k_k, k_a, r_k, hs)


def _tile_rows(x, n):
    return jnp.concatenate([x] * n, axis=0)


def _fold_rows(x, n):
    h = x.shape[0] // n
    acc = x[0:h]
    for j in range(1, n):
        acc = acc + x[j * h:(j + 1) * h]
    return acc


def _same_head_mask():
    rows = lax.broadcasted_iota(jnp.int32, (WKV_PW, WKV_PW), 0)
    cols = lax.broadcasted_iota(jnp.int32, (WKV_PW, WKV_PW), 1)
    head_shift = RWKV_HEAD.bit_length() - 1
    same_head = (rows >> head_shift) == (cols >> head_shift)
    return same_head, rows & (RWKV_HEAD - 1), cols & (RWKV_HEAD - 1), rows == cols


def _wkv_intra_kernel(r_ref, v_ref, kk_ref, lwf_ref, kdf_ref, bf_ref, lwb_ref, kdb_ref, bb_ref,
                      qf_ref, y0f_ref, pf_ref, hf_ref, gf_ref, qb_ref, y0b_ref, pb_ref, hb_ref, gb_ref):
    C, PW, NP = WKV_CHUNK, WKV_PW, WKV_PAIR
    same_head, t_i, s_i, diag = _same_head_mask()
    eye = diag.astype(F32)
    t_c = lax.broadcasted_iota(jnp.int32, (C, C), 0)
    s_c = lax.broadcasted_iota(jnp.int32, (C, C), 1)

    def bd(x):
        return jnp.where(same_head, _tile_rows(x, NP), jnp.zeros((), x.dtype))

    def one_chunk(ci, carry):
        rows = pl.ds(pl.multiple_of(ci * C, C), C)
        r, v, kk = r_ref[rows, :].astype(F32), v_ref[rows, :], kk_ref[rows, :].astype(F32)
        probs = []
        for reverse, (lw_ref, kd_ref, b_ref), out_refs in (
                (False, (lwf_ref, kdf_ref, bf_ref), (qf_ref, y0f_ref, pf_ref, hf_ref, gf_ref)),
                (True, (lwb_ref, kdb_ref, bb_ref), (qb_ref, y0b_ref, pb_ref, hb_ref, gb_ref))):
            cum_mat = ((s_c >= t_c) if reverse else (s_c <= t_c)).astype(BF16)
            earlier = same_head & ((s_i > t_i) if reverse else (s_i < t_i))
            readable = earlier if reverse else (same_head & (s_i <= t_i))
            last_row = 0 if reverse else C - 1

            lw = lw_ref[rows, :]
            lw_hi, lw_lo = _split_bf16(lw)
            cum = (jnp.dot(cum_mat, lw_hi, preferred_element_type=F32)
                   + jnp.dot(cum_mat, lw_lo, preferred_element_type=F32))
            g_ex = jnp.exp(cum - lw)
            g_inv = jnp.exp(-cum)
            g_end = jnp.exp(cum[last_row:last_row + 1, :])
            al_all = -kk * g_ex
            be_all = b_ref[rows, :].astype(F32) * g_inv
            kb_all = kd_ref[rows, :].astype(F32) * g_inv
            rb_all = r * (g_ex if reverse else jnp.exp(cum))
            bt_all = be_all * g_end
            kt_all = kb_all * g_end
            out_refs[4][ci] = g_end
            for p in range(D_RWKV // PW):
                sl = slice(p * PW, (p + 1) * PW)
                probs.append(dict(sl=sl, out=out_refs, earlier=earlier, readable=readable,
                                  al=al_all[:, sl], be=be_all[:, sl], kb=kb_all[:, sl],
                                  rb=rb_all[:, sl], v=v[:, sl], bt=bt_all[:, sl], kt=kt_all[:, sl]))
        _wkv_intra_solve(probs, bd, eye, same_head, rows)
        return carry

    lax.fori_loop(0, WKV_INTRA_CHUNKS, one_chunk, 0)


def _wkv_intra_solve(probs, bd, eye, same_head, rows):
    C, PW, NP = WKV_CHUNK, WKV_PW, WKV_PAIR
    for c in probs:
        c["al_bd"] = bd(c["al"])
        aa = _dot_nt(jnp.concatenate([c["al_bd"], bd(c["rb"])], axis=0),
                     jnp.concatenate([_tile_rows(c["be"], NP), _tile_rows(c["kb"], NP)], axis=0))
        c["a_ab"] = jnp.where(c["earlier"], aa[:PW, :PW], 0.0)
        a_ak = jnp.where(c["earlier"], aa[:PW, PW:], 0.0)
        c["a_rb"] = jnp.where(c["readable"], aa[PW:, :PW], 0.0)
        a_rk = jnp.where(c["readable"], aa[PW:, PW:], 0.0)
        c["a_kk"] = jnp.concatenate([a_ak, a_rk], axis=0)
    for c in probs:
        c["akv"] = _dot(c["a_kk"], bd(c["v"]))
    for c in probs:
        c["inv"] = eye + c["a_ab"]
        c["pk"] = _dot(c["a_ab"], c["a_ab"])
    for _ in range(C.bit_length() - 3):
        for c in probs:
            sq_and_apply = _dot(c["pk"], jnp.concatenate([c["pk"], c["inv"]], axis=1))
            c["inv"] = c["inv"] + sq_and_apply[:, PW:]
            c["pk"] = sq_and_apply[:, :PW]
    for c in probs:
        c["inv"] = c["inv"] + _dot(c["pk"], c["inv"])
    for c in probs:
        c["wu"] = _dot(c["inv"], jnp.concatenate([c["al_bd"], c["akv"][:PW]], axis=1))
    for c in probs:
        c["qy"] = _dot(c["a_rb"], c["wu"])
    for c in probs:
        q_ref, y0_ref, p_ref, h_ref, _ = c["out"]
        sl, bt = c["sl"], c["bt"]
        q_ref[rows, sl] = (c["rb"] + _fold_rows(c["qy"][:, :PW], NP)).astype(BF16)
        y0_ref[rows, sl] = _fold_rows(c["qy"][:, PW:] + c["akv"][PW:], NP).astype(BF16)
        w = _fold_rows(c["wu"][:, :PW], NP)
        u0 = _fold_rows(c["wu"][:, PW:], NP)
        pm = _dot_tn(w, bt)
        hm = _dot_tn(jnp.concatenate([u0.astype(BF16), c["v"]], axis=0),
                     jnp.concatenate([bt, c["kt"]], axis=0))
        p_ref[rows, sl] = _fold_rows(jnp.where(same_head, pm, 0.0), NP).astype(BF16)
        h_ref[rows, sl] = _fold_rows(jnp.where(same_head, hm, 0.0), NP).astype(BF16)


def _wkv_intra(r, v, kk, lwf, kdf, bf, lwb, kdb, bb):
    m = r.shape[0]
    nb = WKV_INTRA_CHUNKS
    n_chunks = m // WKV_CHUNK
    spec = pl.BlockSpec((nb * WKV_CHUNK, D_RWKV), lambda i: (i, 0))
    gspec = pl.BlockSpec((nb, 1, D_RWKV), lambda i: (i, 0, 0))
    f32_tok = jax.ShapeDtypeStruct((m, D_RWKV), F32)
    bf_tok = jax.ShapeDtypeStruct((m, D_RWKV), BF16)
    g_shape = jax.ShapeDtypeStruct((n_chunks, 1, D_RWKV), F32)
    return pl.pallas_call(
        _wkv_intra_kernel, grid=(n_chunks // nb,),
        in_specs=[spec] * 9,
        out_specs=[spec, spec, spec, spec, gspec] * 2,
        out_shape=[bf_tok, bf_tok, bf_tok, bf_tok, g_shape] * 2,
        compiler_params=_cparams(), name="wkv_intra")(r, v, kk, lwf, kdf, bf, lwb, kdb, bb)


def _wkv_inter_kernel(seq, qf_ref, y0f_ref, pf_ref, hf_ref, gf_ref, qb_ref, y0b_ref, pb_ref, hb_ref,
                      gb_ref, yf_ref, yb_ref, s_ref):
    C, PW, NP, NB = WKV_CHUNK, WKV_PW, WKV_PAIR, WKV_INTER_CHUNKS
    n_steps = pl.num_programs(0)
    i = pl.program_id(0)
    same_head = _same_head_mask()[0]

    @pl.when(i == 0)
    def _():
        s_ref[...] = jnp.zeros_like(s_ref)

    dirs = ((qf_ref, y0f_ref, pf_ref, hf_ref, gf_ref, yf_ref),
            (qb_ref, y0b_ref, pb_ref, hb_ref, gb_ref, yb_ref))
    for jj in range(NB):
        chains = []
        for d, (q_ref, y0_ref, p_ref, h_ref, g_ref, y_ref) in enumerate(dirs):
            reverse = d == 1
            blk = (n_steps - 1 - i) if reverse else i
            j = NB - 1 - jj if reverse else jj
            chunk = blk * NB + j
            fresh = _seq_start((chunk + 1) * C if reverse else chunk * C, *seq)
            rs = slice(j * C, (j + 1) * C)
            g = g_ref[j]
            for p in range(D_RWKV // PW):
                sl = slice(p * PW, (p + 1) * PW)
                s0 = jnp.where(fresh, 0.0, s_ref[d, p])
                pm = jnp.where(same_head, _tile_rows(p_ref[rs, sl], NP), 0.0)
                chains.append((d, p, rs, sl, s0, _dot(s0, pm), g[:, sl]))
        for d, p, rs, sl, s0, s0_p, g in chains:
            q_ref, y0_ref, _, h_ref, _, y_ref = dirs[d]
            y_ref[rs, sl] = _dot_nt(q_ref[rs, sl], s0) + y0_ref[rs, sl].astype(F32)
            hm = jnp.where(same_head, _tile_rows(h_ref[rs, sl].astype(F32), NP), 0.0)
            s_ref[d, p] = s0 * g + s0_p + hm


def _wkv_inter(qf, y0f, pf, hf, gf, qb, y0b, pb, hb, gb, seq):
    m = qf.shape[0]
    nb = WKV_INTER_CHUNKS
    tb = nb * WKV_CHUNK
    n_steps = m // tb
    fwd = pl.BlockSpec((tb, D_RWKV), lambda i: (i, 0))
    bwd = pl.BlockSpec((tb, D_RWKV), lambda i: (n_steps - 1 - i, 0))
    gfwd = pl.BlockSpec((nb, 1, D_RWKV), lambda i: (i, 0, 0))
    gbwd = pl.BlockSpec((nb, 1, D_RWKV), lambda i: (n_steps - 1 - i, 0, 0))
    out = jax.ShapeDtypeStruct((m, D_RWKV), F32)
    return pl.pallas_call(
        functools.partial(_wkv_inter_kernel, seq), grid=(n_steps,),
        in_specs=[fwd] * 4 + [gfwd] + [bwd] * 4 + [gbwd],
        out_specs=[fwd, bwd], out_shape=[out, out],
        scratch_shapes=[pltpu.VMEM((2, D_RWKV // WKV_PW, WKV_PW, WKV_PW), F32)],
        compiler_params=_cparams(), name="wkv_inter")(qf, y0f, pf, hf, gf, qb, y0b, pb, hb, gb)


def _post_kernel(tiles_per_group, x_ref, yf_ref, yb_ref, bonus_ref, g_ref, conv_ref, gt_ref, hs_ref,
                 lnxg_ref, lnxb_ref, wrb_ref, wout_ref, bout_ref, ln1g_ref, ln1b_ref,
                 wrh_ref, wrl_ref, br_ref,
                 x1_ref, x1b_ref, tope_ref, gate_ref, rank_ref, count_ref, seen_ref):
    hs = hs_ref[...]
    y = yf_ref[...] + yb_ref[...]
    inv_n = 1.0 / RWKV_HEAD
    mu = _dot(y, hs) * inv_n
    yc = y - mu
    var = _dot(yc * yc, hs) * inv_n
    y = yc * lax.rsqrt(var + GN_EPS) * lnxg_ref[...] + lnxb_ref[...]
    y = (y + bonus_ref[...].astype(F32)) * g_ref[...].astype(F32)
    rwkv_out = _dot(y, wrb_ref[...])
    gt = gt_ref[...]
    mixed = (jax.nn.sigmoid(gt[:, :D_MODEL]) * conv_ref[...]
             + jax.nn.sigmoid(gt[:, D_MODEL:]) * rwkv_out)
    mix = _dot(mixed, wout_ref[...]) + bout_ref[...]
    x1 = _layer_norm(DEEPNORM_ALPHA * x_ref[...] + mix, ln1g_ref[...], ln1b_ref[...], LN_EPS)
    x1_ref[...] = x1
    x1b_ref[...] = x1.astype(BF16)

    xh, xl = _split_bf16(x1)
    logits = (jnp.dot(xh, wrh_ref[...], preferred_element_type=F32)
              + jnp.dot(xh, wrl_ref[...], preferred_element_type=F32)
              + jnp.dot(xl, wrh_ref[...], preferred_element_type=F32)) + br_ref[...]
    lane = lax.broadcasted_iota(jnp.int32, logits.shape, 1).astype(F32)
    neg = jnp.float32(-jnp.inf)
    logits = jnp.where(lane < N_EXPERTS, logits, neg)
    tops, idxs = [], []
    for _ in range(TOP_K):
        mx = jnp.max(logits, axis=-1, keepdims=True)
        ix = jnp.min(jnp.where(logits == mx, lane, float(LANES)), axis=-1, keepdims=True)
        tops.append(mx)
        idxs.append(ix)
        logits = jnp.where(lane == ix, neg, logits)
    es = [jnp.exp(t - tops[0]) for t in tops]
    denom = es[0] + es[1] + es[2] + es[3]
    i = pl.program_id(0)
    tm = logits.shape[0]

    @pl.when(lax.rem(i, jnp.int32(tiles_per_group)) == 0)
    def _():
        seen_ref[...] = jnp.zeros_like(seen_ref)

    onehots = [(lane == ix).astype(F32) for ix in idxs]
    picked = onehots[0] + onehots[1] + onehots[2] + onehots[3]
    t_r = lax.broadcasted_iota(jnp.int32, (tm, tm), 0)
    t_c = lax.broadcasted_iota(jnp.int32, (tm, tm), 1)
    before = _dot((t_c < t_r).astype(BF16), picked) + seen_ref[...]
    tope = jnp.zeros(logits.shape, F32)
    gate = jnp.zeros(logits.shape, F32)
    rank = jnp.zeros(logits.shape, F32)
    for j in range(TOP_K):
        tope = jnp.where(lane == j, idxs[j], tope)
        gate = jnp.where(lane == j, es[j] / denom, gate)
        rank = jnp.where(lane == j, jnp.sum(onehots[j] * before, axis=-1, keepdims=True), rank)
    tope_ref[...] = tope.astype(jnp.int32)
    gate_ref[...] = gate
    rank_ref[...] = rank.astype(jnp.int32)
    seen = seen_ref[...] + jnp.sum(picked, axis=0, keepdims=True)
    seen_ref[...] = seen
    count_ref[...] = jnp.broadcast_to(seen, count_ref.shape).astype(jnp.int32)


def _post(x, yf, yb, bonus, g, conv, gt, hs, lnxg, lnxb, wrb, wout, bout, ln1g, ln1b,
          wrh, wrl, br, layer):
    m = x.shape[0]
    tm = TOK_TILE
    tiles_per_group = m // MOE_GROUPS // tm
    return pl.pallas_call(
        functools.partial(_post_kernel, tiles_per_group), grid=(m // tm,),
        in_specs=[_row_spec(tm, D_MODEL)] + [_row_spec(tm, D_RWKV)] * 4
                 + [_row_spec(tm, D_MODEL), _row_spec(tm, N_GATES), _const_spec((D_RWKV, D_RWKV)),
                    _const_spec((1, D_RWKV), layer), _const_spec((1, D_RWKV), layer),
                    _const_spec((D_RWKV, D_MODEL), layer), _const_spec((D_MODEL, D_MODEL), layer),
                    _const_spec((1, D_MODEL), layer), _const_spec((1, D_MODEL), layer),
                    _const_spec((1, D_MODEL), layer),
                    _const_spec((D_MODEL, LANES), layer), _const_spec((D_MODEL, LANES), layer),
                    _const_spec((1, LANES), layer)],
        out_specs=[_row_spec(tm, D_MODEL), _row_spec(tm, D_MODEL), _row_spec(tm, LANES),
                   _row_spec(tm, LANES), _row_spec(tm, LANES),
                   pl.BlockSpec((SUBLANES, LANES), lambda i: (i // tiles_per_group, 0))],
        out_shape=[jax.ShapeDtypeStruct((m, D_MODEL), F32), jax.ShapeDtypeStruct((m, D_MODEL), BF16),
                   jax.ShapeDtypeStruct((m, LANES), jnp.int32), jax.ShapeDtypeStruct((m, LANES), F32),
                   jax.ShapeDtypeStruct((m, LANES), jnp.int32),
                   jax.ShapeDtypeStruct((MOE_GROUPS * SUBLANES, LANES), jnp.int32)],
        scratch_shapes=[pltpu.VMEM((1, LANES), F32)],
        compiler_params=_cparams(), name="post")(
            x, yf, yb, bonus, g, conv, gt, hs, lnxg, lnxb, wrb, wout, bout, ln1g, ln1b, wrh, wrl, br)


def _moe_kernel(blk_e_ref, nvalid_ref, x_ref, wgu_ref, bgu_ref, wd_ref, bd_ref, o_ref,
                wgu_bf_ref, wd_bf_ref):
    i = pl.program_id(0)
    valid = i < nvalid_ref[0]
    new_expert = (i == 0) | (blk_e_ref[i] != blk_e_ref[jnp.maximum(i - 1, 0)])

    @pl.when(valid & new_expert)
    def _():
        wgu_bf_ref[...] = wgu_ref[...].astype(BF16)
        wd_bf_ref[...] = wd_ref[...].astype(BF16)

    @pl.when(valid)
    def _():
        hgu = jnp.dot(x_ref[...], wgu_bf_ref[...], preferred_element_type=F32) + bgu_ref[...]
        hg = jnp.minimum(hgu[:, :D_EXPERT], SWIGLU_LIMIT)
        hl = jnp.clip(hgu[:, D_EXPERT:], -SWIGLU_LIMIT, SWIGLU_LIMIT)
        act = hg * jax.nn.sigmoid(SWIGLU_ALPHA * hg) * (hl + 1.0)
        o_ref[...] = (_dot(act, wd_bf_ref[...]) + bd_ref[...]).astype(o_ref.dtype)

    @pl.when(i >= nvalid_ref[0])
    def _():
        o_ref[...] = jnp.zeros_like(o_ref)


def _moe(blk_e, nvalid, xs, wgu, bgu, wd, bd, layer):
    n_rows = xs.shape[0]
    tm = MOE_TILE
    grid_spec = pltpu.PrefetchScalarGridSpec(
        num_scalar_prefetch=2, grid=(n_rows // tm,),
        in_specs=[pl.BlockSpec((tm, D_MODEL), lambda i, be, nv: (i, 0)),
                  pl.BlockSpec((None, None, D_MODEL, 2 * D_EXPERT), lambda i, be, nv: (layer, be[i], 0, 0)),
                  pl.BlockSpec((None, None, 1, 2 * D_EXPERT), lambda i, be, nv: (layer, be[i], 0, 0)),
                  pl.BlockSpec((None, None, D_EXPERT, D_MODEL), lambda i, be, nv: (layer, be[i], 0, 0)),
                  pl.BlockSpec((None, None, 1, D_MODEL), lambda i, be, nv: (layer, be[i], 0, 0))],
        out_specs=pl.BlockSpec((tm, D_MODEL), lambda i, be, nv: (i, 0)),
        scratch_shapes=[pltpu.VMEM((D_MODEL, 2 * D_EXPERT), BF16), pltpu.VMEM((D_EXPERT, D_MODEL), BF16)])
    return pl.pallas_call(
        _moe_kernel, grid_spec=grid_spec,
        out_shape=jax.ShapeDtypeStruct((n_rows, D_MODEL), BF16),
        compiler_params=pltpu.CompilerParams(dimension_semantics=("arbitrary",),
                                             vmem_limit_bytes=MOE_VMEM_LIMIT),
        name="moe")(blk_e, nvalid, xs, wgu, bgu, wd, bd)


def _route(top_e, rank_in_e, counts, tile):
    m = top_e.shape[0]
    n_assign = m * TOP_K
    flat_e = top_e.reshape(-1)
    iota = jnp.arange(n_assign, dtype=jnp.int32)
    experts = jnp.arange(N_EXPERTS, dtype=jnp.int32)
    _, order = lax.sort((flat_e, iota), num_keys=1, is_stable=True)
    is_e = flat_e[:, None] == experts[None, :]
    start = jnp.cumsum(counts) - counts
    padded = (counts + tile - 1) // tile * tile
    pend = jnp.cumsum(padded)
    shift = pend - padded - start
    pos = rank_in_e.reshape(-1) + jnp.sum(jnp.where(is_e, (pend - padded)[None, :], 0), axis=1)
    n_blocks = n_assign // tile + N_EXPERTS
    blk_first = jnp.arange(n_blocks, dtype=jnp.int32) * tile
    blk_e = jnp.minimum(jnp.sum(pend[None, :] <= blk_first[:, None], axis=1, dtype=jnp.int32),
                        N_EXPERTS - 1)
    src = (blk_first - shift[blk_e])[:, None] + jnp.arange(tile, dtype=jnp.int32)[None, :]
    valid = src < (start + counts)[blk_e][:, None]
    row_tok = jnp.where(valid, order[jnp.clip(src, 0, n_assign - 1)] // TOP_K, 0).reshape(-1)
    nvalid = (pend[-1:] // tile).astype(jnp.int32)
    return row_tok, pos.reshape(m, TOP_K), blk_e, nvalid


def _final_kernel(n_first, x1_ref, yga_ref, ygb_ref, gate_ref, p_ref, wpg_ref, bpg_ref, wpp_ref,
                  g_ref, b_ref, o_ref):
    x1 = x1_ref[...]
    gate = gate_ref[...]
    in_first = pl.program_id(0) < n_first
    ffn = jnp.zeros(x1.shape, F32)
    for j in range(TOP_K):
        yj = jnp.where(in_first, yga_ref[j], ygb_ref[j]).astype(F32)
        ffn = ffn + gate[:, j:j + 1] * yj
    ple = jax.nn.sigmoid(_dot(x1, wpg_ref[...]) + bpg_ref[...]) * _dot(p_ref[...], wpp_ref[...])
    o_ref[...] = _layer_norm(DEEPNORM_ALPHA * x1 + ffn + ple, g_ref[...], b_ref[...], LN_EPS)


def _final(x1, yg_a, yg_b, gate, p, wpg, bpg, wpp, g, b, layer):
    m = x1.shape[0]
    tm = TOK_TILE
    n_first = yg_a.shape[1] // tm
    return pl.pallas_call(
        functools.partial(_final_kernel, n_first), grid=(m // tm,),
        in_specs=[_row_spec(tm, D_MODEL),
                  pl.BlockSpec((TOP_K, tm, D_MODEL), lambda i: (0, jnp.minimum(i, n_first - 1), 0)),
                  pl.BlockSpec((TOP_K, tm, D_MODEL), lambda i: (0, jnp.maximum(i - n_first, 0), 0)),
                  _row_spec(tm, LANES),
                  pl.BlockSpec((None, tm, D_PLE), lambda i: (layer, i, 0)),
                  _const_spec((D_MODEL, D_MODEL), layer), _const_spec((1, D_MODEL), layer),
                  _const_spec((D_PLE, D_MODEL), layer), _const_spec((1, D_MODEL), layer),
                  _const_spec((1, D_MODEL), layer)],
        out_specs=_row_spec(tm, D_MODEL),
        out_shape=jax.ShapeDtypeStruct((m, D_MODEL), F32),
        compiler_params=_cparams(), name="final")(x1, yg_a, yg_b, gate, p, wpg, bpg, wpp, g, b)


def _pad_lowrank(w):
    z = jnp.zeros_like(w[:, 0])
    fwd = jnp.concatenate([w[:, 0], z], axis=1)
    bwd = jnp.concatenate([z, w[:, 1]], axis=1)
    return jnp.stack([fwd, bwd], axis=1)


def kernel(x_prompt, x_sample, p_prompt, p_sample, ln0_g, ln0_b, w_in, b_in, mu_shift, dw_w, dw_b, cln_g, cln_b, w_cb, b_cb, w0, w2, a0, a2, g2, k_k, k_a, r_k, lnx_g, lnx_b, w_rb, w_out, b_out, ln1_g, ln1_b, w_router, b_router, w_gu, b_gu, w_down, b_down, w_pg, b_pg, w_pp, ln2_g, ln2_b):
    bp, tp, d = x_prompt.shape
    bs, ts, _ = x_sample.shape
    mp, ms = bp * tp, bs * ts
    m = mp + ms
    assert d == D_MODEL and tp % TOK_TILE == 0 and ts % TOK_TILE == 0
    assert m % LN0_TILE == 0 and m % (MOE_GROUPS * TOK_TILE) == 0
    assert (m // MOE_GROUPS * TOP_K) % MOE_TILE == 0
    assert m % (WKV_INTRA_CHUNKS * WKV_CHUNK) == 0 and m % (WKV_INTER_CHUNKS * WKV_CHUNK) == 0
    seq = (mp, tp, ts, m)
    L = w_in.shape[0]

    x = jnp.concatenate([x_prompt.reshape(mp, d), x_sample.reshape(ms, d)], axis=0)
    p = jnp.concatenate([p_prompt.reshape(L, mp, D_PLE), p_sample.reshape(L, ms, D_PLE)], axis=1)

    row = lambda a: a.reshape(a.shape[0], 1, -1)
    wc = w_in[:, :, :N_GLU].astype(BF16)
    wz = w_in[:, :, N_GLU:N_GLU + N_SHIFT].astype(BF16)
    wg = w_in[:, :, N_GLU + N_SHIFT:].astype(BF16)
    bc, bz, bg = row(b_in[:, :N_GLU]), row(b_in[:, N_GLU:N_GLU + N_SHIFT]), row(b_in[:, N_GLU + N_SHIFT:])
    w2p = _pad_lowrank(w2).astype(BF16)
    a2p = _pad_lowrank(a2).astype(BF16)
    head = jnp.arange(D_RWKV, dtype=jnp.int32) // RWKV_HEAD
    hs = (head[:, None] == head[None, :]).astype(BF16)
    wr_pad = jnp.pad(w_router, ((0, 0), (0, 0), (0, LANES - N_EXPERTS)))
    wrh = wr_pad.astype(BF16)
    wrl = (wr_pad - wrh.astype(F32)).astype(BF16)
    br = row(jnp.pad(b_router, ((0, 0), (0, LANES - N_EXPERTS))))
    bgu = b_gu.reshape(L, N_EXPERTS, 1, 2 * D_EXPERT)
    bdn = b_down.reshape(L, N_EXPERTS, 1, D_MODEL)
    wcb_b, wrb_b, wout_b = w_cb.astype(BF16), w_rb.astype(BF16), w_out.astype(BF16)
    wpg_b, wpp_b, g2_b = w_pg.astype(BF16), w_pp.astype(BF16), g2.astype(BF16)

    x = _ln0(x, ln0_g, ln0_b)
    for l in range(L):
        c, z, gt = _in_proj(x, wc, wz, wg, bc, bz, bg, l)
        conv = _conv_branch(c, dw_w, row(dw_b), row(cln_g), row(cln_b), wcb_b, row(b_cb), l, seq)
        (r, v, kk, g, bonus, lwf, kdf, bf, lwb, kdb, bb) = _rwkv_prep(
            z, mu_shift, w0, w2p, a0, a2p, g2_b, row(k_k), row(k_a), row(r_k.reshape(L, D_RWKV)), hs, l, seq)
        yf, yb = _wkv_inter(*_wkv_intra(r, v, kk, lwf, kdf, bf, lwb, kdb, bb), seq)
        x1, x1b, tope, gate, rank, counts = _post(
            x, yf, yb, bonus, g, conv, gt, hs, row(lnx_g), row(lnx_b), wrb_b, wout_b, row(b_out),
            row(ln1_g), row(ln1_b), wrh, wrl, br, l)
        ygs = []
        mg = m // MOE_GROUPS
        for h in range(MOE_GROUPS):
            lo = h * mg
            row_tok, pos, blk_e, nvalid = _route(
                tope[lo:lo + mg, :TOP_K], rank[lo:lo + mg, :TOP_K], counts[h * SUBLANES, :N_EXPERTS],
                MOE_TILE)
            y_rows = _moe(blk_e, nvalid, x1b[row_tok + lo], w_gu, bgu, w_down, bdn, l)
            ygs.append(y_rows[pos.T])
        x = _final(x1, ygs[0], ygs[1], gate, p, wpg_b, row(b_pg), wpp_b, row(ln2_g), row(ln2_b), l)

    return (x[:mp].reshape(bp, tp, d), x[mp:].reshape(bs, ts, d))
```

```python
import functools
import math

import jax
import jax.numpy as jnp
from jax import lax
from jax.experimental import pallas as pl
from jax.experimental.pallas import tpu as pltpu

F32 = jnp.float32
BF16 = jnp.bfloat16

D_MODEL = 1024
DEPTH = 4
D_CONV = 512
CONV_WIDTH = 31
CONV_HALO = 16
D_RWKV = 512
RWKV_HEAD = 64
DECAY_RANK = 64
ICLR_RANK = 64
GATE_RANK = 128
N_EXPERTS = 32
TOP_K = 4
D_EXPERT = 1024
SWIGLU_LIMIT = 7.0
SWIGLU_ALPHA = 1.702
D_PLE = 256
LN_EPS = 1e-5
GN_EPS = 64e-5
DEEPNORM_ALPHA = (2 * DEPTH) ** 0.25
DECAY_SCALE = math.exp(-0.5)
N_GLU = 2 * D_CONV
N_SHIFT = 3 * D_RWKV + 2 * DECAY_RANK + 2 * ICLR_RANK + GATE_RANK
N_GATES = 2 * D_MODEL
N_LOWRANK = 2 * DECAY_RANK

LANES = 128
SUBLANES = 8
SHIFT_HALO = SUBLANES
TOK_TILE = 256
LN0_TILE = 512
MOE_TILE = 512
MOE_GROUPS = 2
WKV_CHUNK = 64
WKV_PAIR = 2
WKV_PW = WKV_PAIR * RWKV_HEAD
WKV_INTER_CHUNKS = 4
WKV_INTRA_CHUNKS = 4
VMEM_LIMIT = 48 * 1024 * 1024
MOE_VMEM_LIMIT = 58 * 1024 * 1024
assert WKV_CHUNK == RWKV_HEAD


def _cparams(n_axes=1):
    return pltpu.CompilerParams(dimension_semantics=("arbitrary",) * n_axes,
                                vmem_limit_bytes=VMEM_LIMIT)


def _dot(a, b):
    return jnp.dot(a.astype(BF16), b.astype(BF16), preferred_element_type=F32)


def _dot_nt(a, b):
    return lax.dot_general(a.astype(BF16), b.astype(BF16), (((1,), (1,)), ((), ())),
                           preferred_element_type=F32)


def _dot_tn(a, b):
    return lax.dot_general(a.astype(BF16), b.astype(BF16), (((0,), (0,)), ((), ())),
                           preferred_element_type=F32)


def _split_bf16(x):
    hi = x.astype(BF16)
    lo = (x - hi.astype(F32)).astype(BF16)
    return hi, lo


def _layer_norm(x, g, b, eps):
    mu = jnp.mean(x, axis=-1, keepdims=True)
    xc = x - mu
    var = jnp.mean(xc * xc, axis=-1, keepdims=True)
    return xc * lax.rsqrt(var + eps) * g + b


def _seq_start(tok, m_prompt, t_prompt, t_sample, m_total):
    in_prompt = tok < m_prompt
    at_prompt_start = lax.rem(tok, jnp.int32(t_prompt)) == 0
    at_sample_start = lax.rem(jnp.abs(tok - m_prompt), jnp.int32(t_sample)) == 0
    return ((in_prompt & at_prompt_start) | (jnp.logical_not(in_prompt) & at_sample_start)
            | (tok >= m_total))


def _const_spec(shape, layer=None):
    if layer is None:
        nd = len(shape)
        return pl.BlockSpec(shape, lambda i, _n=nd: (0,) * _n)
    nd = len(shape)
    return pl.BlockSpec((None,) + tuple(shape), lambda i, _l=layer, _n=nd: (_l,) + (0,) * _n)


def _row_spec(tm, width, first_tile=0):
    return pl.BlockSpec((tm, width), lambda i: (i + first_tile, 0))


def _ln_kernel(x_ref, g_ref, b_ref, o_ref):
    o_ref[...] = _layer_norm(x_ref[...], g_ref[...], b_ref[...], LN_EPS)


def _ln0(x, g, b):
    m = x.shape[0]
    tm = LN0_TILE
    return pl.pallas_call(
        _ln_kernel, grid=(m // tm,),
        in_specs=[_row_spec(tm, D_MODEL), _const_spec((1, D_MODEL)), _const_spec((1, D_MODEL))],
        out_specs=_row_spec(tm, D_MODEL),
        out_shape=jax.ShapeDtypeStruct((m, D_MODEL), F32),
        compiler_params=_cparams(), name="ln0")(x, g.reshape(1, -1), b.reshape(1, -1))


def _in_proj_kernel(x_ref, wc_ref, wz_ref, wg_ref, bc_ref, bz_ref, bg_ref, c_ref, z_ref, gt_ref):
    xb = x_ref[...].astype(BF16)
    c_ref[...] = jnp.dot(xb, wc_ref[...], preferred_element_type=F32) + bc_ref[...]
    z_ref[...] = jnp.dot(xb, wz_ref[...], preferred_element_type=F32) + bz_ref[...]
    gt_ref[...] = jnp.dot(xb, wg_ref[...], preferred_element_type=F32) + bg_ref[...]


def _in_proj(x, wc, wz, wg, bc, bz, bg, layer):
    m = x.shape[0]
    tm = TOK_TILE
    return pl.pallas_call(
        _in_proj_kernel, grid=(m // tm,),
        in_specs=[_row_spec(tm, D_MODEL),
                  _const_spec((D_MODEL, N_GLU), layer), _const_spec((D_MODEL, N_SHIFT), layer),
                  _const_spec((D_MODEL, N_GATES), layer),
                  _const_spec((1, N_GLU), layer), _const_spec((1, N_SHIFT), layer),
                  _const_spec((1, N_GATES), layer)],
        out_specs=[_row_spec(tm, N_GLU), _row_spec(tm, N_SHIFT), _row_spec(tm, N_GATES)],
        out_shape=[jax.ShapeDtypeStruct((m, N_GLU), F32), jax.ShapeDtypeStruct((m, N_SHIFT), F32),
                   jax.ShapeDtypeStruct((m, N_GATES), F32)],
        compiler_params=_cparams(), name="in_proj")(x, wc, wz, wg, bc, bz, bg)


def _conv_kernel(seq, c_ref, cp_ref, cn_ref, dww_ref, dwb_ref, g_ref, b_ref, wcb_ref, bcb_ref,
                 o_ref, u_ref, sh_ref):
    tm = c_ref.shape[0]
    i = pl.program_id(0)
    first = _seq_start(i * tm, *seq)
    last = _seq_start((i + 1) * tm, *seq)

    def glu(c):
        return c[:, :D_CONV] * jax.nn.sigmoid(c[:, D_CONV:])

    u_ref[0:CONV_HALO, :] = jnp.where(first, 0.0, glu(cp_ref[...]))
    u_ref[CONV_HALO:CONV_HALO + tm, :] = glu(c_ref[...])
    u_ref[CONV_HALO + tm:, :] = jnp.where(last, 0.0, glu(cn_ref[...]))

    span = sh_ref.shape[1]
    for s in range(SUBLANES):
        sh_ref[s] = u_ref[s:s + span, :]
    base = CONV_HALO - CONV_WIDTH // 2
    acc = jnp.zeros((tm, D_CONV), F32) + dwb_ref[...]
    for j in range(CONV_WIDTH):
        a, s = divmod(base + j, SUBLANES)
        acc = acc + sh_ref[s, a * SUBLANES:a * SUBLANES + tm, :] * dww_ref[j:j + 1, :]
    u = _layer_norm(acc, g_ref[...], b_ref[...], LN_EPS)
    u = u * jax.nn.sigmoid(u)
    o_ref[...] = _dot(u, wcb_ref[...]) + bcb_ref[...]


def _conv_branch(c, dww, dwb, g, b, wcb, bcb, layer, seq):
    m = c.shape[0]
    tm = TOK_TILE
    r = tm // CONV_HALO
    nh = m // CONV_HALO
    return pl.pallas_call(
        functools.partial(_conv_kernel, seq), grid=(m // tm,),
        in_specs=[_row_spec(tm, N_GLU),
                  pl.BlockSpec((CONV_HALO, N_GLU), lambda i: (jnp.maximum(i * r - 1, 0), 0)),
                  pl.BlockSpec((CONV_HALO, N_GLU), lambda i: (jnp.minimum((i + 1) * r, nh - 1), 0)),
                  _const_spec((CONV_WIDTH, D_CONV), layer), _const_spec((1, D_CONV), layer),
                  _const_spec((1, D_CONV), layer), _const_spec((1, D_CONV), layer),
                  _const_spec((D_CONV, D_MODEL), layer), _const_spec((1, D_MODEL), layer)],
        out_specs=_row_spec(tm, D_MODEL),
        out_shape=jax.ShapeDtypeStruct((m, D_MODEL), F32),
        scratch_shapes=[pltpu.VMEM((tm + 2 * CONV_HALO, D_CONV), F32),
                        pltpu.VMEM((SUBLANES, tm + 2 * CONV_HALO - SUBLANES, D_CONV), F32)],
        compiler_params=_cparams(), name="conv_branch")(c, c, c, dww, dwb, g, b, wcb, bcb)


def _rwkv_prep_kernel(seq, z_ref, zp_ref, zn_ref, mu_ref, w0_ref, w2_ref, a0_ref, a2_ref, g2_ref,
                      kk_ref, ka_ref, rk_ref, hs_ref,
                      r_ref, v_ref, kkn_ref, g_ref, bonus_ref,
                      lwf_ref, kdf_ref, bf_ref, lwb_ref, kdb_ref, bb_ref):
    tm = z_ref.shape[0]
    i = pl.program_id(0)
    first = _seq_start(i * tm, *seq)
    last = _seq_start((i + 1) * tm, *seq)

    z = z_ref[...]
    row = lax.broadcasted_iota(jnp.int32, (tm, 1), 0)
    prev_row = jnp.where(first, 0.0, zp_ref[SHIFT_HALO - 1:SHIFT_HALO, :])
    next_row = jnp.where(last, 0.0, zn_ref[0:1, :])
    zp = jnp.where(row == 0, prev_row, pltpu.roll(z, 1, 0))
    zn = jnp.where(row == tm - 1, next_row, pltpu.roll(z, tm - 1, 0))
    zs = z + mu_ref[0:1, :] * (zp - z) + mu_ref[1:2, :] * (zn - z)

    o = 3 * D_RWKV
    r = zs[:, 0:D_RWKV]
    k = zs[:, D_RWKV:2 * D_RWKV]
    v = zs[:, 2 * D_RWKV:o]
    lw = zs[:, o:o + N_LOWRANK]
    la = zs[:, o + N_LOWRANK:o + 2 * N_LOWRANK]
    lg = zs[:, o + 2 * N_LOWRANK:]

    hs = hs_ref[...]
    kk = k * kk_ref[...]
    kk = kk * lax.rsqrt(_dot(kk * kk, hs) + 1e-12)
    r_ref[...] = r.astype(BF16)
    v_ref[...] = v.astype(BF16)
    kkn_ref[...] = kk.astype(BF16)
    g_ref[...] = _dot(jax.nn.sigmoid(lg), g2_ref[...]).astype(BF16)
    bonus_ref[...] = (_dot(r * k * rk_ref[...], hs) * v).astype(BF16)

    tlw = jnp.tanh(lw)
    ka = ka_ref[...]
    for d, (lw_o, kd_o, b_o) in enumerate(((lwf_ref, kdf_ref, bf_ref), (lwb_ref, kdb_ref, bb_ref))):
        t = w0_ref[d:d + 1, :] + _dot(tlw, w2_ref[d])
        a = jax.nn.sigmoid(a0_ref[d:d + 1, :] + _dot(la, a2_ref[d]))
        lw_o[...] = -DECAY_SCALE * jax.nn.sigmoid(t)
        kd_o[...] = (k * (1.0 + (a - 1.0) * ka)).astype(BF16)
        b_o[...] = (kk * a).astype(BF16)


def _rwkv_prep(z, mu, w0, w2p, a0, a2p, g2, k_k, k_a, r_k, hs, layer, seq):
    m = z.shape[0]
    tm = TOK_TILE
    r = tm // SHIFT_HALO
    nh = m // SHIFT_HALO
    f32_tok = jax.ShapeDtypeStruct((m, D_RWKV), F32)
    bf_tok = jax.ShapeDtypeStruct((m, D_RWKV), BF16)
    outs = [bf_tok] * 5 + [f32_tok, bf_tok, bf_tok] * 2
    return pl.pallas_call(
        functools.partial(_rwkv_prep_kernel, seq), grid=(m // tm,),
        in_specs=[_row_spec(tm, N_SHIFT),
                  pl.BlockSpec((SHIFT_HALO, N_SHIFT), lambda i: (jnp.maximum(i * r - 1, 0), 0)),
                  pl.BlockSpec((SHIFT_HALO, N_SHIFT), lambda i: (jnp.minimum((i + 1) * r, nh - 1), 0)),
                  _const_spec((2, N_SHIFT), layer), _const_spec((2, D_RWKV), layer),
                  _const_spec((2, N_LOWRANK, D_RWKV), layer), _const_spec((2, D_RWKV), layer),
                  _const_spec((2, N_LOWRANK, D_RWKV), layer), _const_spec((GATE_RANK, D_RWKV), layer),
                  _const_spec((1, D_RWKV), layer), _const_spec((1, D_RWKV), layer),
                  _const_spec((1, D_RWKV), layer), _const_spec((D_RWKV, D_RWKV))],
        out_specs=[_row_spec(tm, D_RWKV)] * 11,
        out_shape=outs,
        compiler_params=_cparams(), name="rwkv_prep")(z, z, z, mu, w0, w2p, a0, a2p, g2, k_k, k_a, r_k, hs)


def _tile_rows(x, n):
    return jnp.concatenate([x] * n, axis=0)


def _fold_rows(x, n):
    h = x.shape[0] // n
    acc = x[0:h]
    for j in range(1, n):
        acc = acc + x[j * h:(j + 1) * h]
    return acc


def _same_head_mask():
    rows = lax.broadcasted_iota(jnp.int32, (WKV_PW, WKV_PW), 0)
    cols = lax.broadcasted_iota(jnp.int32, (WKV_PW, WKV_PW), 1)
    head_shift = RWKV_HEAD.bit_length() - 1
    same_head = (rows >> head_shift) == (cols >> head_shift)
    return same_head, rows & (RWKV_HEAD - 1), cols & (RWKV_HEAD - 1), rows == cols


def _wkv_intra_kernel(r_ref, v_ref, kk_ref, lwf_ref, kdf_ref, bf_ref, lwb_ref, kdb_ref, bb_ref,
                      qf_ref, y0f_ref, pf_ref, hf_ref, gf_ref, qb_ref, y0b_ref, pb_ref, hb_ref, gb_ref):
    C, PW, NP = WKV_CHUNK, WKV_PW, WKV_PAIR
    same_head, t_i, s_i, diag = _same_head_mask()
    eye = diag.astype(F32)
    t_c = lax.broadcasted_iota(jnp.int32, (C, C), 0)
    s_c = lax.broadcasted_iota(jnp.int32, (C, C), 1)

    def bd(x):
        return jnp.where(same_head, _tile_rows(x, NP), jnp.zeros((), x.dtype))

    def one_chunk(ci, carry):
        rows = pl.ds(pl.multiple_of(ci * C, C), C)
        r, v, kk = r_ref[rows, :].astype(F32), v_ref[rows, :], kk_ref[rows, :].astype(F32)
        probs = []
        for reverse, (lw_ref, kd_ref, b_ref), out_refs in (
                (False, (lwf_ref, kdf_ref, bf_ref), (qf_ref, y0f_ref, pf_ref, hf_ref, gf_ref)),
                (True, (lwb_ref, kdb_ref, bb_ref), (qb_ref, y0b_ref, pb_ref, hb_ref, gb_ref))):
            cum_mat = ((s_c >= t_c) if reverse else (s_c <= t_c)).astype(BF16)
            earlier = same_head & ((s_i > t_i) if reverse else (s_i < t_i))
            readable = earlier if reverse else (same_head & (s_i <= t_i))
            last_row = 0 if reverse else C - 1

            lw = lw_ref[rows, :]
            lw_hi, lw_lo = _split_bf16(lw)
            cum = (jnp.dot(cum_mat, lw_hi, preferred_element_type=F32)
                   + jnp.dot(cum_mat, lw_lo, preferred_element_type=F32))
            g_ex = jnp.exp(cum - lw)
            g_inv = jnp.exp(-cum)
            g_end = jnp.exp(cum[last_row:last_row + 1, :])
            al_all = -kk * g_ex
            be_all = b_ref[rows, :].astype(F32) * g_inv
            kb_all = kd_ref[rows, :].astype(F32) * g_inv
            rb_all = r * (g_ex if reverse else jnp.exp(cum))
            bt_all = be_all * g_end
            kt_all = kb_all * g_end
            out_refs[4][ci] = g_end
            for p in range(D_RWKV // PW):
                sl = slice(p * PW, (p + 1) * PW)
                probs.append(dict(sl=sl, out=out_refs, earlier=earlier, readable=readable,
                                  al=al_all[:, sl], be=be_all[:, sl], kb=kb_all[:, sl],
                                  rb=rb_all[:, sl], v=v[:, sl], bt=bt_all[:, sl], kt=kt_all[:, sl]))
        _wkv_intra_solve(probs, bd, eye, same_head, rows)
        return carry

    lax.fori_loop(0, WKV_INTRA_CHUNKS, one_chunk, 0)


def _wkv_intra_solve(probs, bd, eye, same_head, rows):
    C, PW, NP = WKV_CHUNK, WKV_PW, WKV_PAIR
    for c in probs:
        c["al_bd"] = bd(c["al"])
        aa = _dot_nt(jnp.concatenate([c["al_bd"], bd(c["rb"])], axis=0),
                     jnp.concatenate([_tile_rows(c["be"], NP), _tile_rows(c["kb"], NP)], axis=0))
        c["a_ab"] = jnp.where(c["earlier"], aa[:PW, :PW], 0.0)
        a_ak = jnp.where(c["earlier"], aa[:PW, PW:], 0.0)
        c["a_rb"] = jnp.where(c["readable"], aa[PW:, :PW], 0.0)
        a_rk = jnp.where(c["readable"], aa[PW:, PW:], 0.0)
        c["a_kk"] = jnp.concatenate([a_ak, a_rk], axis=0)
    for c in probs:
        c["akv"] = _dot(c["a_kk"], bd(c["v"]))
    for c in probs:
        c["inv"] = eye + c["a_ab"]
        c["pk"] = _dot(c["a_ab"], c["a_ab"])
    for _ in range(C.bit_length() - 3):
        for c in probs:
            sq_and_apply = _dot(c["pk"], jnp.concatenate([c["pk"], c["inv"]], axis=1))
            c["inv"] = c["inv"] + sq_and_apply[:, PW:]
            c["pk"] = sq_and_apply[:, :PW]
    for c in probs:
        c["inv"] = c["inv"] + _dot(c["pk"], c["inv"])
    for c in probs:
        c["wu"] = _dot(c["inv"], jnp.concatenate([c["al_bd"], c["akv"][:PW]], axis=1))
    for c in probs:
        c["qy"] = _dot(c["a_rb"], c["wu"])
    for c in probs:
        q_ref, y0_ref, p_ref, h_ref, _ = c["out"]
        sl, bt = c["sl"], c["bt"]
        q_ref[rows, sl] = (c["rb"] + _fold_rows(c["qy"][:, :PW], NP)).astype(BF16)
        y0_ref[rows, sl] = _fold_rows(c["qy"][:, PW:] + c["akv"][PW:], NP).astype(BF16)
        w = _fold_rows(c["wu"][:, :PW], NP)
        u0 = _fold_rows(c["wu"][:, PW:], NP)
        pm = _dot_tn(w, bt)
        hm = _dot_tn(jnp.concatenate([u0.astype(BF16), c["v"]], axis=0),
                     jnp.concatenate([bt, c["kt"]], axis=0))
        p_ref[rows, sl] = _fold_rows(jnp.where(same_head, pm, 0.0), NP).astype(BF16)
        h_ref[rows, sl] = _fold_rows(jnp.where(same_head, hm, 0.0), NP).astype(BF16)


def _wkv_intra(r, v, kk, lwf, kdf, bf, lwb, kdb, bb):
    m = r.shape[0]
    nb = WKV_INTRA_CHUNKS
    n_chunks = m // WKV_CHUNK
    spec = pl.BlockSpec((nb * WKV_CHUNK, D_RWKV), lambda i: (i, 0))
    gspec = pl.BlockSpec((nb, 1, D_RWKV), lambda i: (i, 0, 0))
    f32_tok = jax.ShapeDtypeStruct((m, D_RWKV), F32)
    bf_tok = jax.ShapeDtypeStruct((m, D_RWKV), BF16)
    g_shape = jax.ShapeDtypeStruct((n_chunks, 1, D_RWKV), F32)
    return pl.pallas_call(
        _wkv_intra_kernel, grid=(n_chunks // nb,),
        in_specs=[spec] * 9,
        out_specs=[spec, spec, spec, spec, gspec] * 2,
        out_shape=[bf_tok, bf_tok, bf_tok, bf_tok, g_shape] * 2,
        compiler_params=_cparams(), name="wkv_intra")(r, v, kk, lwf, kdf, bf, lwb, kdb, bb)


def _wkv_inter_kernel(seq, qf_ref, y0f_ref, pf_ref, hf_ref, gf_ref, qb_ref, y0b_ref, pb_ref, hb_ref,
                      gb_ref, yf_ref, yb_ref, s_ref):
    C, PW, NP, NB = WKV_CHUNK, WKV_PW, WKV_PAIR, WKV_INTER_CHUNKS
    n_steps = pl.num_programs(0)
    i = pl.program_id(0)
    same_head = _same_head_mask()[0]

    @pl.when(i == 0)
    def _():
        s_ref[...] = jnp.zeros_like(s_ref)

    dirs = ((qf_ref, y0f_ref, pf_ref, hf_ref, gf_ref, yf_ref),
            (qb_ref, y0b_ref, pb_ref, hb_ref, gb_ref, yb_ref))
    for jj in range(NB):
        chains = []
        for d, (q_ref, y0_ref, p_ref, h_ref, g_ref, y_ref) in enumerate(dirs):
            reverse = d == 1
            blk = (n_steps - 1 - i) if reverse else i
            j = NB - 1 - jj if reverse else jj
            chunk = blk * NB + j
            fresh = _seq_start((chunk + 1) * C if reverse else chunk * C, *seq)
            rs = slice(j * C, (j + 1) * C)
            g = g_ref[j]
            for p in range(D_RWKV // PW):
                sl = slice(p * PW, (p + 1) * PW)
                s0 = jnp.where(fresh, 0.0, s_ref[d, p])
                pm = jnp.where(same_head, _tile_rows(p_ref[rs, sl], NP), 0.0)
                chains.append((d, p, rs, sl, s0, _dot(s0, pm), g[:, sl]))
        for d, p, rs, sl, s0, s0_p, g in chains:
            q_ref, y0_ref, _, h_ref, _, y_ref = dirs[d]
            y_ref[rs, sl] = _dot_nt(q_ref[rs, sl], s0) + y0_ref[rs, sl].astype(F32)
            hm = jnp.where(same_head, _tile_rows(h_ref[rs, sl].astype(F32), NP), 0.0)
            s_ref[d, p] = s0 * g + s0_p + hm


def _wkv_inter(qf, y0f, pf, hf, gf, qb, y0b, pb, hb, gb, seq):
    m = qf.shape[0]
    nb = WKV_INTER_CHUNKS
    tb = nb * WKV_CHUNK
    n_steps = m // tb
    fwd = pl.BlockSpec((tb, D_RWKV), lambda i: (i, 0))
    bwd = pl.BlockSpec((tb, D_RWKV), lambda i: (n_steps - 1 - i, 0))
    gfwd = pl.BlockSpec((nb, 1, D_RWKV), lambda i: (i, 0, 0))
    gbwd = pl.BlockSpec((nb, 1, D_RWKV), lambda i: (n_steps - 1 - i, 0, 0))
    out = jax.ShapeDtypeStruct((m, D_RWKV), F32)
    return pl.pallas_call(
        functools.partial(_wkv_inter_kernel, seq), grid=(n_steps,),
        in_specs=[fwd] * 4 + [gfwd] + [bwd] * 4 + [gbwd],
        out_specs=[fwd, bwd], out_shape=[out, out],
        scratch_shapes=[pltpu.VMEM((2, D_RWKV // WKV_PW, WKV_PW, WKV_PW), F32)],
        compiler_params=_cparams(), name="wkv_inter")(qf, y0f, pf, hf, gf, qb, y0b, pb, hb, gb)


def _post_kernel(x_ref, yf_ref, yb_ref, bonus_ref, g_ref, conv_ref, gt_ref, hs_ref,
                 lnxg_ref, lnxb_ref, wrb_ref, wout_ref, bout_ref, ln1g_ref, ln1b_ref,
                 wrh_ref, wrl_ref, br_ref,
                 x1_ref, x1b_ref, tope_ref, gate_ref, rank_ref, count_ref, seen_ref):
    hs = hs_ref[...]
    y = yf_ref[...] + yb_ref[...]
    inv_n = 1.0 / RWKV_HEAD
    mu = _dot(y, hs) * inv_n
    yc = y - mu
    var = _dot(yc * yc, hs) * inv_n
    y = yc * lax.rsqrt(var + GN_EPS) * lnxg_ref[...] + lnxb_ref[...]
    y = (y + bonus_ref[...].astype(F32)) * g_ref[...].astype(F32)
    rwkv_out = _dot(y, wrb_ref[...])
    gt = gt_ref[...]
    mixed = (jax.nn.sigmoid(gt[:, :D_MODEL]) * conv_ref[...]
             + jax.nn.sigmoid(gt[:, D_MODEL:]) * rwkv_out)
    mix = _dot(mixed, wout_ref[...]) + bout_ref[...]
    x1 = _layer_norm(DEEPNORM_ALPHA * x_ref[...] + mix, ln1g_ref[...], ln1b_ref[...], LN_EPS)
    x1_ref[...] = x1
    x1b_ref[...] = x1.astype(BF16)

    xh, xl = _split_bf16(x1)
    logits = (jnp.dot(xh, wrh_ref[...], preferred_element_type=F32)
              + jnp.dot(xh, wrl_ref[...], preferred_element_type=F32)
              + jnp.dot(xl, wrh_ref[...], preferred_element_type=F32)) + br_ref[...]
    lane = lax.broadcasted_iota(jnp.int32, logits.shape, 1).astype(F32)
    neg = jnp.float32(-jnp.inf)
    logits = jnp.where(lane < N_EXPERTS, logits, neg)
    tops, idxs = [], []
    for _ in range(TOP_K):
        mx = jnp.max(logits, axis=-1, keepdims=True)
        ix = jnp.min(jnp.where(logits == mx, lane, float(LANES)), axis=-1, keepdims=True)
        tops.append(mx)
        idxs.append(ix)
        logits = jnp.where(lane == ix, neg, logits)
    es = [jnp.exp(t - tops[0]) for t in tops]
    denom = es[0] + es[1] + es[2] + es[3]
    i = pl.program_id(0)
    tm = logits.shape[0]

    @pl.when(i == 0)
    def _():
        seen_ref[...] = jnp.zeros_like(seen_ref)

    onehots = [(lane == ix).astype(F32) for ix in idxs]
    picked = onehots[0] + onehots[1] + onehots[2] + onehots[3]
    t_r = lax.broadcasted_iota(jnp.int32, (tm, tm), 0)
    t_c = lax.broadcasted_iota(jnp.int32, (tm, tm), 1)
    before = _dot((t_c < t_r).astype(BF16), picked) + seen_ref[...]
    tope = jnp.zeros(logits.shape, F32)
    gate = jnp.zeros(logits.shape, F32)
    rank = jnp.zeros(logits.shape, F32)
    for j in range(TOP_K):
        tope = jnp.where(lane == j, idxs[j], tope)
        gate = jnp.where(lane == j, es[j] / denom, gate)
        rank = jnp.where(lane == j, jnp.sum(onehots[j] * before, axis=-1, keepdims=True), rank)
    tope_ref[...] = tope.astype(jnp.int32)
    gate_ref[...] = gate
    rank_ref[...] = rank.astype(jnp.int32)
    seen = seen_ref[...] + jnp.sum(picked, axis=0, keepdims=True)
    seen_ref[...] = seen
    count_ref[...] = jnp.broadcast_to(seen, count_ref.shape).astype(jnp.int32)


def _post(x, yf, yb, bonus, g, conv, gt, hs, lnxg, lnxb, wrb, wout, bout, ln1g, ln1b,
          wrh, wrl, br, layer, group):
    tm = TOK_TILE
    m = x.shape[0] // MOE_GROUPS
    ft = group * (m // tm)
    return pl.pallas_call(
        _post_kernel, grid=(m // tm,),
        in_specs=[_row_spec(tm, D_MODEL, ft)] + [_row_spec(tm, D_RWKV, ft)] * 4
                 + [_row_spec(tm, D_MODEL, ft), _row_spec(tm, N_GATES, ft), _const_spec((D_RWKV, D_RWKV)),
                    _const_spec((1, D_RWKV), layer), _const_spec((1, D_RWKV), layer),
                    _const_spec((D_RWKV, D_MODEL), layer), _const_spec((D_MODEL, D_MODEL), layer),
                    _const_spec((1, D_MODEL), layer), _const_spec((1, D_MODEL), layer),
                    _const_spec((1, D_MODEL), layer),
                    _const_spec((D_MODEL, LANES), layer), _const_spec((D_MODEL, LANES), layer),
                    _const_spec((1, LANES), layer)],
        out_specs=[_row_spec(tm, D_MODEL), _row_spec(tm, D_MODEL), _row_spec(tm, LANES),
                   _row_spec(tm, LANES), _row_spec(tm, LANES),
                   pl.BlockSpec((SUBLANES, LANES), lambda i: (0, 0))],
        out_shape=[jax.ShapeDtypeStruct((m, D_MODEL), F32), jax.ShapeDtypeStruct((m, D_MODEL), BF16),
                   jax.ShapeDtypeStruct((m, LANES), jnp.int32), jax.ShapeDtypeStruct((m, LANES), F32),
                   jax.ShapeDtypeStruct((m, LANES), jnp.int32),
                   jax.ShapeDtypeStruct((SUBLANES, LANES), jnp.int32)],
        scratch_shapes=[pltpu.VMEM((1, LANES), F32)],
        compiler_params=_cparams(), name="post")(
            x, yf, yb, bonus, g, conv, gt, hs, lnxg, lnxb, wrb, wout, bout, ln1g, ln1b, wrh, wrl, br)


def _moe_kernel(blk_e_ref, nvalid_ref, x_ref, wgu_ref, bgu_ref, wd_ref, bd_ref, o_ref,
                wgu_bf_ref, wd_bf_ref):
    i = pl.program_id(0)
    valid = i < nvalid_ref[0]
    new_expert = (i == 0) | (blk_e_ref[i] != blk_e_ref[jnp.maximum(i - 1, 0)])

    @pl.when(valid & new_expert)
    def _():
        wgu_bf_ref[...] = wgu_ref[...].astype(BF16)
        wd_bf_ref[...] = wd_ref[...].astype(BF16)

    @pl.when(valid)
    def _():
        hgu = jnp.dot(x_ref[...], wgu_bf_ref[...], preferred_element_type=F32) + bgu_ref[...]
        hg = jnp.minimum(hgu[:, :D_EXPERT], SWIGLU_LIMIT)
        hl = jnp.clip(hgu[:, D_EXPERT:], -SWIGLU_LIMIT, SWIGLU_LIMIT)
        act = hg * jax.nn.sigmoid(SWIGLU_ALPHA * hg) * (hl + 1.0)
        o_ref[...] = (_dot(act, wd_bf_ref[...]) + bd_ref[...]).astype(o_ref.dtype)

    @pl.when(i >= nvalid_ref[0])
    def _():
        o_ref[...] = jnp.zeros_like(o_ref)


def _moe(blk_e, nvalid, xs, wgu, bgu, wd, bd, layer):
    n_rows = xs.shape[0]
    tm = MOE_TILE
    grid_spec = pltpu.PrefetchScalarGridSpec(
        num_scalar_prefetch=2, grid=(n_rows // tm,),
        in_specs=[pl.BlockSpec((tm, D_MODEL), lambda i, be, nv: (i, 0)),
                  pl.BlockSpec((None, None, D_MODEL, 2 * D_EXPERT), lambda i, be, nv: (layer, be[i], 0, 0)),
                  pl.BlockSpec((None, None, 1, 2 * D_EXPERT), lambda i, be, nv: (layer, be[i], 0, 0)),
                  pl.BlockSpec((None, None, D_EXPERT, D_MODEL), lambda i, be, nv: (layer, be[i], 0, 0)),
                  pl.BlockSpec((None, None, 1, D_MODEL), lambda i, be, nv: (layer, be[i], 0, 0))],
        out_specs=pl.BlockSpec((tm, D_MODEL), lambda i, be, nv: (i, 0)),
        scratch_shapes=[pltpu.VMEM((D_MODEL, 2 * D_EXPERT), BF16), pltpu.VMEM((D_EXPERT, D_MODEL), BF16)])
    return pl.pallas_call(
        _moe_kernel, grid_spec=grid_spec,
        out_shape=jax.ShapeDtypeStruct((n_rows, D_MODEL), BF16),
        compiler_params=pltpu.CompilerParams(dimension_semantics=("arbitrary",),
                                             vmem_limit_bytes=MOE_VMEM_LIMIT),
        name="moe")(blk_e, nvalid, xs, wgu, bgu, wd, bd)


def _route(top_e, rank_in_e, counts, tile):
    m = top_e.shape[0]
    n_assign = m * TOP_K
    flat_e = top_e.reshape(-1)
    iota = jnp.arange(n_assign, dtype=jnp.int32)
    experts = jnp.arange(N_EXPERTS, dtype=jnp.int32)
    _, order = lax.sort((flat_e, iota), num_keys=1, is_stable=True)
    is_e = flat_e[:, None] == experts[None, :]
    start = jnp.cumsum(counts) - counts
    padded = (counts + tile - 1) // tile * tile
    pend = jnp.cumsum(padded)
    shift = pend - padded - start
    pos = rank_in_e.reshape(-1) + jnp.sum(jnp.where(is_e, (pend - padded)[None, :], 0), axis=1)
    n_blocks = n_assign // tile + N_EXPERTS
    blk_first = jnp.arange(n_blocks, dtype=jnp.int32) * tile
    blk_e = jnp.minimum(jnp.sum(pend[None, :] <= blk_first[:, None], axis=1, dtype=jnp.int32),
                        N_EXPERTS - 1)
    src = (blk_first - shift[blk_e])[:, None] + jnp.arange(tile, dtype=jnp.int32)[None, :]
    valid = src < (start + counts)[blk_e][:, None]
    row_tok = jnp.where(valid, order[jnp.clip(src, 0, n_assign - 1)] // TOP_K, 0).reshape(-1)
    nvalid = (pend[-1:] // tile).astype(jnp.int32)
    return row_tok, pos.reshape(m, TOP_K), blk_e, nvalid


def _final_kernel(x1_ref, yg_ref, gate_ref, p_ref, wpg_ref, bpg_ref, wpp_ref, g_ref, b_ref, *rest):
    o_ref = rest[-1]
    x1 = x1_ref[...]
    gate = gate_ref[...]
    ffn = jnp.zeros(x1.shape, F32)
    for j in range(TOP_K):
        ffn = ffn + gate[:, j:j + 1] * yg_ref[j].astype(F32)
    ple = jax.nn.sigmoid(_dot(x1, wpg_ref[...]) + bpg_ref[...]) * _dot(p_ref[...], wpp_ref[...])
    o_ref[...] = _layer_norm(DEEPNORM_ALPHA * x1 + ffn + ple, g_ref[...], b_ref[...], LN_EPS)


def _final(x1, yg, gate, p, wpg, bpg, wpp, g, b, layer, group, out_prev):
    tm = TOK_TILE
    mg = x1.shape[0]
    ft = group * (mg // tm)
    in_specs = [_row_spec(tm, D_MODEL),
                pl.BlockSpec((TOP_K, tm, D_MODEL), lambda i: (0, i, 0)),
                _row_spec(tm, LANES),
                pl.BlockSpec((None, tm, D_PLE), lambda i: (layer, i + ft, 0)),
                _const_spec((D_MODEL, D_MODEL), layer), _const_spec((1, D_MODEL), layer),
                _const_spec((D_PLE, D_MODEL), layer), _const_spec((1, D_MODEL), layer),
                _const_spec((1, D_MODEL), layer)]
    args = [x1, yg, gate, p, wpg, bpg, wpp, g, b]
    aliases = {}
    if out_prev is not None:
        in_specs.append(pl.BlockSpec(memory_space=pl.ANY))
        args.append(out_prev)
        aliases = {len(args) - 1: 0}
    return pl.pallas_call(
        _final_kernel, grid=(mg // tm,),
        in_specs=in_specs,
        out_specs=_row_spec(tm, D_MODEL, ft),
        out_shape=jax.ShapeDtypeStruct((mg * MOE_GROUPS, D_MODEL), F32),
        input_output_aliases=aliases,
        compiler_params=_cparams(), name="final")(*args)


def _pad_lowrank(w):
    z = jnp.zeros_like(w[:, 0])
    fwd = jnp.concatenate([w[:, 0], z], axis=1)
    bwd = jnp.concatenate([z, w[:, 1]], axis=1)
    return jnp.stack([fwd, bwd], axis=1)


def kernel(x_prompt, x_sample, p_prompt, p_sample, ln0_g, ln0_b, w_in, b_in, mu_shift, dw_w, dw_b, cln_g, cln_b, w_cb, b_cb, w0, w2, a0, a2, g2, k_k, k_a, r_k, lnx_g, lnx_b, w_rb, w_out, b_out, ln1_g, ln1_b, w_router, b_router, w_gu, b_gu, w_down, b_down, w_pg, b_pg, w_pp, ln2_g, ln2_b):
    bp, tp, d = x_prompt.shape
    bs, ts, _ = x_sample.shape
    mp, ms = bp * tp, bs * ts
    m = mp + ms
    assert d == D_MODEL and tp % TOK_TILE == 0 and ts % TOK_TILE == 0
    assert m % LN0_TILE == 0 and m % (MOE_GROUPS * TOK_TILE) == 0
    assert (m // MOE_GROUPS * TOP_K) % MOE_TILE == 0
    assert m % (WKV_INTRA_CHUNKS * WKV_CHUNK) == 0 and m % (WKV_INTER_CHUNKS * WKV_CHUNK) == 0
    seq = (mp, tp, ts, m)
    L = w_in.shape[0]

    x = jnp.concatenate([x_prompt.reshape(mp, d), x_sample.reshape(ms, d)], axis=0)
    p = jnp.concatenate([p_prompt.reshape(L, mp, D_PLE), p_sample.reshape(L, ms, D_PLE)], axis=1)

    row = lambda a: a.reshape(a.shape[0], 1, -1)
    wc = w_in[:, :, :N_GLU].astype(BF16)
    wz = w_in[:, :, N_GLU:N_GLU + N_SHIFT].astype(BF16)
    wg = w_in[:, :, N_GLU + N_SHIFT:].astype(BF16)
    bc, bz, bg = row(b_in[:, :N_GLU]), row(b_in[:, N_GLU:N_GLU + N_SHIFT]), row(b_in[:, N_GLU + N_SHIFT:])
    w2p = _pad_lowrank(w2).astype(BF16)
    a2p = _pad_lowrank(a2).astype(BF16)
    head = jnp.arange(D_RWKV, dtype=jnp.int32) // RWKV_HEAD
    hs = (head[:, None] == head[None, :]).astype(BF16)
    wr_pad = jnp.pad(w_router, ((0, 0), (0, 0), (0, LANES - N_EXPERTS)))
    wrh = wr_pad.astype(BF16)
    wrl = (wr_pad - wrh.astype(F32)).astype(BF16)
    br = row(jnp.pad(b_router, ((0, 0), (0, LANES - N_EXPERTS))))
    bgu = b_gu.reshape(L, N_EXPERTS, 1, 2 * D_EXPERT)
    bdn = b_down.reshape(L, N_EXPERTS, 1, D_MODEL)
    wcb_b, wrb_b, wout_b = w_cb.astype(BF16), w_rb.astype(BF16), w_out.astype(BF16)
    wpg_b, wpp_b, g2_b = w_pg.astype(BF16), w_pp.astype(BF16), g2.astype(BF16)

    x = _ln0(x, ln0_g, ln0_b)
    for l in range(L):
        c, z, gt = _in_proj(x, wc, wz, wg, bc, bz, bg, l)
        conv = _conv_branch(c, dw_w, row(dw_b), row(cln_g), row(cln_b), wcb_b, row(b_cb), l, seq)
        (r, v, kk, g, bonus, lwf, kdf, bf, lwb, kdb, bb) = _rwkv_prep(
            z, mu_shift, w0, w2p, a0, a2p, g2_b, row(k_k), row(k_a), row(r_k.reshape(L, D_RWKV)), hs, l, seq)
        yf, yb = _wkv_inter(*_wkv_intra(r, v, kk, lwf, kdf, bf, lwb, kdb, bb), seq)
        x_next = None
        for h in range(MOE_GROUPS):
            x1, x1b, tope, gate, rank, counts = _post(
                x, yf, yb, bonus, g, conv, gt, hs, row(lnx_g), row(lnx_b), wrb_b, wout_b, row(b_out),
                row(ln1_g), row(ln1_b), wrh, wrl, br, l, h)
            row_tok, pos, blk_e, nvalid = _route(
                tope[:, :TOP_K], rank[:, :TOP_K], counts[0, :N_EXPERTS], MOE_TILE)
            y_rows = _moe(blk_e, nvalid, x1b[row_tok], w_gu, bgu, w_down, bdn, l)
            x_next = _final(x1, y_rows[pos.T], gate, p, wpg_b, row(b_pg), wpp_b, row(ln2_g),
                            row(ln2_b), l, h, x_next)
        x = x_next

    return (x[:mp].reshape(bp, tp, d), x[mp:].reshape(bs, ts, d))
```

```python
import functools
import math

import jax
import jax.numpy as jnp
from jax import lax
from jax.experimental import pallas as pl
from jax.experimental.pallas import tpu as pltpu

F32 = jnp.float32
BF16 = jnp.bfloat16

D_MODEL = 1024
DEPTH = 4
D_CONV = 512
CONV_WIDTH = 31
CONV_HALO = 16
D_RWKV = 512
RWKV_HEAD = 64
DECAY_RANK = 64
ICLR_RANK = 64
GATE_RANK = 128
N_EXPERTS = 32
TOP_K = 4
D_EXPERT = 1024
SWIGLU_LIMIT = 7.0
SWIGLU_ALPHA = 1.702
D_PLE = 256
LN_EPS = 1e-5
GN_EPS = 64e-5
DEEPNORM_ALPHA = (2 * DEPTH) ** 0.25
DECAY_SCALE = math.exp(-0.5)
N_GLU = 2 * D_CONV
N_SHIFT = 3 * D_RWKV + 2 * DECAY_RANK + 2 * ICLR_RANK + GATE_RANK
N_GATES = 2 * D_MODEL
N_LOWRANK = 2 * DECAY_RANK

LANES = 128
SUBLANES = 8
SHIFT_HALO = SUBLANES
TOK_TILE = 256
LN0_TILE = 512
MOE_TILE = 512
MOE_GROUPS = 2
WKV_CHUNK = 64
WKV_PAIR = 2
WKV_PW = WKV_PAIR * RWKV_HEAD
WKV_INTER_CHUNKS = 4
WKV_INTRA_CHUNKS = 4
VMEM_LIMIT = 48 * 1024 * 1024
MOE_VMEM_LIMIT = 58 * 1024 * 1024
assert WKV_CHUNK == RWKV_HEAD


def _cparams(n_axes=1):
    return pltpu.CompilerParams(dimension_semantics=("arbitrary",) * n_axes,
                                vmem_limit_bytes=VMEM_LIMIT)


def _dot(a, b):
    return jnp.dot(a.astype(BF16), b.astype(BF16), preferred_element_type=F32)


def _dot_nt(a, b):
    return lax.dot_general(a.astype(BF16), b.astype(BF16), (((1,), (1,)), ((), ())),
                           preferred_element_type=F32)


def _dot_tn(a, b):
    return lax.dot_general(a.astype(BF16), b.astype(BF16), (((0,), (0,)), ((), ())),
                           preferred_element_type=F32)


def _split_bf16(x):
    hi = x.astype(BF16)
    lo = (x - hi.astype(F32)).astype(BF16)
    return hi, lo


def _layer_norm(x, g, b, eps):
    mu = jnp.mean(x, axis=-1, keepdims=True)
    xc = x - mu
    var = jnp.mean(xc * xc, axis=-1, keepdims=True)
    return xc * lax.rsqrt(var + eps) * g + b


def _seq_start(tok, m_prompt, t_prompt, t_sample, m_total):
    in_prompt = tok < m_prompt
    at_prompt_start = lax.rem(tok, jnp.int32(t_prompt)) == 0
    at_sample_start = lax.rem(jnp.abs(tok - m_prompt), jnp.int32(t_sample)) == 0
    return ((in_prompt & at_prompt_start) | (jnp.logical_not(in_prompt) & at_sample_start)
            | (tok >= m_total))


def _const_spec(shape, layer=None):
    if layer is None:
        nd = len(shape)
        return pl.BlockSpec(shape, lambda i, _n=nd: (0,) * _n)
    nd = len(shape)
    return pl.BlockSpec((None,) + tuple(shape), lambda i, _l=layer, _n=nd: (_l,) + (0,) * _n)


def _row_spec(tm, width, first_tile=0):
    return pl.BlockSpec((tm, width), lambda i: (i + first_tile, 0))


def _ln_kernel(x_ref, g_ref, b_ref, o_ref):
    o_ref[...] = _layer_norm(x_ref[...], g_ref[...], b_ref[...], LN_EPS)


def _ln0(x, g, b):
    m = x.shape[0]
    tm = LN0_TILE
    return pl.pallas_call(
        _ln_kernel, grid=(m // tm,),
        in_specs=[_row_spec(tm, D_MODEL), _const_spec((1, D_MODEL)), _const_spec((1, D_MODEL))],
        out_specs=_row_spec(tm, D_MODEL),
        out_shape=jax.ShapeDtypeStruct((m, D_MODEL), F32),
        compiler_params=_cparams(), name="ln0")(x, g.reshape(1, -1), b.reshape(1, -1))


def _in_proj_kernel(x_ref, wc_ref, wz_ref, wg_ref, bc_ref, bz_ref, bg_ref, c_ref, z_ref, gt_ref):
    xb = x_ref[...].astype(BF16)
    c_ref[...] = jnp.dot(xb, wc_ref[...], preferred_element_type=F32) + bc_ref[...]
    z_ref[...] = jnp.dot(xb, wz_ref[...], preferred_element_type=F32) + bz_ref[...]
    gt_ref[...] = jnp.dot(xb, wg_ref[...], preferred_element_type=F32) + bg_ref[...]


def _in_proj(x, wc, wz, wg, bc, bz, bg, layer):
    m = x.shape[0]
    tm = TOK_TILE
    return pl.pallas_call(
        _in_proj_kernel, grid=(m // tm,),
        in_specs=[_row_spec(tm, D_MODEL),
                  _const_spec((D_MODEL, N_GLU), layer), _const_spec((D_MODEL, N_SHIFT), layer),
                  _const_spec((D_MODEL, N_GATES), layer),
                  _const_spec((1, N_GLU), layer), _const_spec((1, N_SHIFT), layer),
                  _const_spec((1, N_GATES), layer)],
        out_specs=[_row_spec(tm, N_GLU), _row_spec(tm, N_SHIFT), _row_spec(tm, N_GATES)],
        out_shape=[jax.ShapeDtypeStruct((m, N_GLU), F32), jax.ShapeDtypeStruct((m, N_SHIFT), F32),
                   jax.ShapeDtypeStruct((m, N_GATES), F32)],
        compiler_params=_cparams(), name="in_proj")(x, wc, wz, wg, bc, bz, bg)


def _conv_kernel(seq, c_ref, cp_ref, cn_ref, dww_ref, dwb_ref, g_ref, b_ref, wcb_ref, bcb_ref,
                 o_ref, u_ref, sh_ref):
    tm = c_ref.shape[0]
    i = pl.program_id(0)
    first = _seq_start(i * tm, *seq)
    last = _seq_start((i + 1) * tm, *seq)

    def glu(c):
        return c[:, :D_CONV] * jax.nn.sigmoid(c[:, D_CONV:])

    u_ref[0:CONV_HALO, :] = jnp.where(first, 0.0, glu(cp_ref[...]))
    u_ref[CONV_HALO:CONV_HALO + tm, :] = glu(c_ref[...])
    u_ref[CONV_HALO + tm:, :] = jnp.where(last, 0.0, glu(cn_ref[...]))

    span = sh_ref.shape[1]
    for s in range(SUBLANES):
        sh_ref[s] = u_ref[s:s + span, :]
    base = CONV_HALO - CONV_WIDTH // 2
    acc = jnp.zeros((tm, D_CONV), F32) + dwb_ref[...]
    for j in range(CONV_WIDTH):
        a, s = divmod(base + j, SUBLANES)
        acc = acc + sh_ref[s, a * SUBLANES:a * SUBLANES + tm, :] * dww_ref[j:j + 1, :]
    u = _layer_norm(acc, g_ref[...], b_ref[...], LN_EPS)
    u = u * jax.nn.sigmoid(u)
    o_ref[...] = _dot(u, wcb_ref[...]) + bcb_ref[...]


def _conv_branch(c, dww, dwb, g, b, wcb, bcb, layer, seq):
    m = c.shape[0]
    tm = TOK_TILE
    r = tm // CONV_HALO
    nh = m // CONV_HALO
    return pl.pallas_call(
        functools.partial(_conv_kernel, seq), grid=(m // tm,),
        in_specs=[_row_spec(tm, N_GLU),
                  pl.BlockSpec((CONV_HALO, N_GLU), lambda i: (jnp.maximum(i * r - 1, 0), 0)),
                  pl.BlockSpec((CONV_HALO, N_GLU), lambda i: (jnp.minimum((i + 1) * r, nh - 1), 0)),
                  _const_spec((CONV_WIDTH, D_CONV), layer), _const_spec((1, D_CONV), layer),
                  _const_spec((1, D_CONV), layer), _const_spec((1, D_CONV), layer),
                  _const_spec((D_CONV, D_MODEL), layer), _const_spec((1, D_MODEL), layer)],
        out_specs=_row_spec(tm, D_MODEL),
        out_shape=jax.ShapeDtypeStruct((m, D_MODEL), F32),
        scratch_shapes=[pltpu.VMEM((tm + 2 * CONV_HALO, D_CONV), F32),
                        pltpu.VMEM((SUBLANES, tm + 2 * CONV_HALO - SUBLANES, D_CONV), F32)],
        compiler_params=_cparams(), name="conv_branch")(c, c, c, dww, dwb, g, b, wcb, bcb)


def _rwkv_prep_kernel(seq, z_ref, zp_ref, zn_ref, mu_ref, w0_ref, w2_ref, a0_ref, a2_ref, g2_ref,
                      kk_ref, ka_ref, rk_ref, hs_ref,
                      r_ref, v_ref, kkn_ref, g_ref, bonus_ref,
                      lwf_ref, kdf_ref, bf_ref, lwb_ref, kdb_ref, bb_ref):
    tm = z_ref.shape[0]
    i = pl.program_id(0)
    first = _seq_start(i * tm, *seq)
    last = _seq_start((i + 1) * tm, *seq)

    z = z_ref[...]
    row = lax.broadcasted_iota(jnp.int32, (tm, 1), 0)
    prev_row = jnp.where(first, 0.0, zp_ref[SHIFT_HALO - 1:SHIFT_HALO, :])
    next_row = jnp.where(last, 0.0, zn_ref[0:1, :])
    zp = jnp.where(row == 0, prev_row, pltpu.roll(z, 1, 0))
    zn = jnp.where(row == tm - 1, next_row, pltpu.roll(z, tm - 1, 0))
    zs = z + mu_ref[0:1, :] * (zp - z) + mu_ref[1:2, :] * (zn - z)

    o = 3 * D_RWKV
    r = zs[:, 0:D_RWKV]
    k = zs[:, D_RWKV:2 * D_RWKV]
    v = zs[:, 2 * D_RWKV:o]
    lw = zs[:, o:o + N_LOWRANK]
    la = zs[:, o + N_LOWRANK:o + 2 * N_LOWRANK]
    lg = zs[:, o + 2 * N_LOWRANK:]

    hs = hs_ref[...]
    kk = k * kk_ref[...]
    kk = kk * lax.rsqrt(_dot(kk * kk, hs) + 1e-12)
    r_ref[...] = r.astype(BF16)
    v_ref[...] = v.astype(BF16)
    kkn_ref[...] = kk.astype(BF16)
    g_ref[...] = _dot(jax.nn.sigmoid(lg), g2_ref[...]).astype(BF16)
    bonus_ref[...] = (_dot(r * k * rk_ref[...], hs) * v).astype(BF16)

    tlw = jnp.tanh(lw)
    ka = ka_ref[...]
    for d, (lw_o, kd_o, b_o) in enumerate(((lwf_ref, kdf_ref, bf_ref), (lwb_ref, kdb_ref, bb_ref))):
        t = w0_ref[d:d + 1, :] + _dot(tlw, w2_ref[d])
        a = jax.nn.sigmoid(a0_ref[d:d + 1, :] + _dot(la, a2_ref[d]))
        lw_o[...] = -DECAY_SCALE * jax.nn.sigmoid(t)
        kd_o[...] = (k * (1.0 + (a - 1.0) * ka)).astype(BF16)
        b_o[...] = (kk * a).astype(BF16)


def _rwkv_prep(z, mu, w0, w2p, a0, a2p, g2, k_k, k_a, r_k, hs, layer, seq):
    m = z.shape[0]
    tm = TOK_TILE
    r = tm // SHIFT_HALO
    nh = m // SHIFT_HALO
    f32_tok = jax.ShapeDtypeStruct((m, D_RWKV), F32)
    bf_tok = jax.ShapeDtypeStruct((m, D_RWKV), BF16)
    outs = [bf_tok] * 5 + [f32_tok, bf_tok, bf_tok] * 2
    return pl.pallas_call(
        functools.partial(_rwkv_prep_kernel, seq), grid=(m // tm,),
        in_specs=[_row_spec(tm, N_SHIFT),
                  pl.BlockSpec((SHIFT_HALO, N_SHIFT), lambda i: (jnp.maximum(i * r - 1, 0), 0)),
                  pl.BlockSpec((SHIFT_HALO, N_SHIFT), lambda i: (jnp.minimum((i + 1) * r, nh - 1), 0)),
                  _const_spec((2, N_SHIFT), layer), _const_spec((2, D_RWKV), layer),
                  _const_spec((2, N_LOWRANK, D_RWKV), layer), _const_spec((2, D_RWKV), layer),
                  _const_spec((2, N_LOWRANK, D_RWKV), layer), _const_spec((GATE_RANK, D_RWKV), layer),
                  _const_spec((1, D_RWKV), layer), _const_spec((1, D_RWKV), layer),
                  _const_spec((1, D_RWKV), layer), _const_spec((D_RWKV, D_RWKV))],
        out_specs=[_row_spec(tm, D_RWKV)] * 11,
        out_shape=outs,
        compiler_params=_cparams(), name="rwkv_prep")(z, z, z, mu, w0, w2p, a0, a2p, g2, k_k, k_a, r_k, hs)


def _tile_rows(x, n):
    return jnp.concatenate([x] * n, axis=0)


def _fold_rows(x, n):
    h = x.shape[0] // n
    acc = x[0:h]
    for j in range(1, n):
        acc = acc + x[j * h:(j + 1) * h]
    return acc


def _same_head_mask():
    rows = lax.broadcasted_iota(jnp.int32, (WKV_PW, WKV_PW), 0)
    cols = lax.broadcasted_iota(jnp.int32, (WKV_PW, WKV_PW), 1)
    head_shift = RWKV_HEAD.bit_length() - 1
    same_head = (rows >> head_shift) == (cols >> head_shift)
    return same_head, rows & (RWKV_HEAD - 1), cols & (RWKV_HEAD - 1), rows == cols


def _wkv_intra_kernel(r_ref, v_ref, kk_ref, lwf_ref, kdf_ref, bf_ref, lwb_ref, kdb_ref, bb_ref,
                      qf_ref, y0f_ref, pf_ref, hf_ref, gf_ref, qb_ref, y0b_ref, pb_ref, hb_ref, gb_ref):
    C, PW, NP = WKV_CHUNK, WKV_PW, WKV_PAIR
    same_head, t_i, s_i, diag = _same_head_mask()
    eye = diag.astype(F32)
    t_c = lax.broadcasted_iota(jnp.int32, (C, C), 0)
    s_c = lax.broadcasted_iota(jnp.int32, (C, C), 1)

    def bd(x):
        return jnp.where(same_head, _tile_rows(x, NP), jnp.zeros((), x.dtype))

    def one_chunk(ci, carry):
        rows = pl.ds(pl.multiple_of(ci * C, C), C)
        r, v, kk = r_ref[rows, :].astype(F32), v_ref[rows, :], kk_ref[rows, :].astype(F32)
        probs = []
        for reverse, (lw_ref, kd_ref, b_ref), out_refs in (
                (False, (lwf_ref, kdf_ref, bf_ref), (qf_ref, y0f_ref, pf_ref, hf_ref, gf_ref)),
                (True, (lwb_ref, kdb_ref, bb_ref), (qb_ref, y0b_ref, pb_ref, hb_ref, gb_ref))):
            cum_mat = ((s_c >= t_c) if reverse else (s_c <= t_c)).astype(BF16)
            earlier = same_head & ((s_i > t_i) if reverse else (s_i < t_i))
            readable = earlier if reverse else (same_head & (s_i <= t_i))
            last_row = 0 if reverse else C - 1

            lw = lw_ref[rows, :]
            lw_hi, lw_lo = _split_bf16(lw)
            cum = (jnp.dot(cum_mat, lw_hi, preferred_element_type=F32)
                   + jnp.dot(cum_mat, lw_lo, preferred_element_type=F32))
            g_ex = jnp.exp(cum - lw)
            g_inv = jnp.exp(-cum)
            g_end = jnp.exp(cum[last_row:last_row + 1, :])
            al_all = -kk * g_ex
            be_all = b_ref[rows, :].astype(F32) * g_inv
            kb_all = kd_ref[rows, :].astype(F32) * g_inv
            rb_all = r * (g_ex if reverse else jnp.exp(cum))
            bt_all = be_all * g_end
            kt_all = kb_all * g_end
            out_refs[4][ci] = g_end
            for p in range(D_RWKV // PW):
                sl = slice(p * PW, (p + 1) * PW)
                probs.append(dict(sl=sl, out=out_refs, earlier=earlier, readable=readable,
                                  al=al_all[:, sl], be=be_all[:, sl], kb=kb_all[:, sl],
                                  rb=rb_all[:, sl], v=v[:, sl], bt=bt_all[:, sl], kt=kt_all[:, sl]))
        _wkv_intra_solve(probs, bd, eye, same_head, rows)
        return carry

    lax.fori_loop(0, WKV_INTRA_CHUNKS, one_chunk, 0)


def _wkv_intra_solve(probs, bd, eye, same_head, rows):
    C, PW, NP = WKV_CHUNK, WKV_PW, WKV_PAIR
    for c in probs:
        c["al_bd"] = bd(c["al"])
        aa = _dot_nt(jnp.concatenate([c["al_bd"], bd(c["rb"])], axis=0),
                     jnp.concatenate([_tile_rows(c["be"], NP), _tile_rows(c["kb"], NP)], axis=0))
        c["a_ab"] = jnp.where(c["earlier"], aa[:PW, :PW], 0.0)
        a_ak = jnp.where(c["earlier"], aa[:PW, PW:], 0.0)
        c["a_rb"] = jnp.where(c["readable"], aa[PW:, :PW], 0.0)
        a_rk = jnp.where(c["readable"], aa[PW:, PW:], 0.0)
        c["a_kk"] = jnp.concatenate([a_ak, a_rk], axis=0)
    for c in probs:
        c["akv"] = _dot(c["a_kk"], bd(c["v"]))
    for c in probs:
        c["inv"] = eye + c["a_ab"]
        c["pk"] = _dot(c["a_ab"], c["a_ab"])
    for _ in range(C.bit_length() - 3):
        for c in probs:
            sq_and_apply = _dot(c["pk"], jnp.concatenate([c["pk"], c["inv"]], axis=1))
            c["inv"] = c["inv"] + sq_and_apply[:, PW:]
            c["pk"] = sq_and_apply[:, :PW]
    for c in probs:
        c["inv"] = c["inv"] + _dot(c["pk"], c["inv"])
    for c in probs:
        c["wu"] = _dot(c["inv"], jnp.concatenate([c["al_bd"], c["akv"][:PW]], axis=1))
    for c in probs:
        c["qy"] = _dot(c["a_rb"], c["wu"])
    for c in probs:
        q_ref, y0_ref, p_ref, h_ref, _ = c["out"]
        sl, bt = c["sl"], c["bt"]
        q_ref[rows, sl] = (c["rb"] + _fold_rows(c["qy"][:, :PW], NP)).astype(BF16)
        y0_ref[rows, sl] = _fold_rows(c["qy"][:, PW:] + c["akv"][PW:], NP).astype(BF16)
        w = _fold_rows(c["wu"][:, :PW], NP)
        u0 = _fold_rows(c["wu"][:, PW:], NP)
        pm = _dot_tn(w, bt)
        hm = _dot_tn(jnp.concatenate([u0.astype(BF16), c["v"]], axis=0),
                     jnp.concatenate([bt, c["kt"]], axis=0))
        p_ref[rows, sl] = _fold_rows(jnp.where(same_head, pm, 0.0), NP).astype(BF16)
        h_ref[rows, sl] = _fold_rows(jnp.where(same_head, hm, 0.0), NP).astype(BF16)


def _wkv_intra(r, v, kk, lwf, kdf, bf, lwb, kdb, bb):
    m = r.shape[0]
    nb = WKV_INTRA_CHUNKS
    n_chunks = m // WKV_CHUNK
    spec = pl.BlockSpec((nb * WKV_CHUNK, D_RWKV), lambda i: (i, 0))
    gspec = pl.BlockSpec((nb, 1, D_RWKV), lambda i: (i, 0, 0))
    f32_tok = jax.ShapeDtypeStruct((m, D_RWKV), F32)
    bf_tok = jax.ShapeDtypeStruct((m, D_RWKV), BF16)
    g_shape = jax.ShapeDtypeStruct((n_chunks, 1, D_RWKV), F32)
    return pl.pallas_call(
        _wkv_intra_kernel, grid=(n_chunks // nb,),
        in_specs=[spec] * 9,
        out_specs=[spec, spec, spec, spec, gspec] * 2,
        out_shape=[bf_tok, bf_tok, bf_tok, bf_tok, g_shape] * 2,
        compiler_params=_cparams(), name="wkv_intra")(r, v, kk, lwf, kdf, bf, lwb, kdb, bb)


def _wkv_inter_kernel(seq, qf_ref, y0f_ref, pf_ref, hf_ref, gf_ref, qb_ref, y0b_ref, pb_ref, hb_ref,
                      gb_ref, yf_ref, yb_ref, s_ref):
    C, PW, NP, NB = WKV_CHUNK, WKV_PW, WKV_PAIR, WKV_INTER_CHUNKS
    n_steps = pl.num_programs(0)
    i = pl.program_id(0)
    same_head = _same_head_mask()[0]

    @pl.when(i == 0)
    def _():
        s_ref[...] = jnp.zeros_like(s_ref)

    dirs = ((qf_ref, y0f_ref, pf_ref, hf_ref, gf_ref, yf_ref),
            (qb_ref, y0b_ref, pb_ref, hb_ref, gb_ref, yb_ref))
    for jj in range(NB):
        chains = []
        for d, (q_ref, y0_ref, p_ref, h_ref, g_ref, y_ref) in enumerate(dirs):
            reverse = d == 1
            blk = (n_steps - 1 - i) if reverse else i
            j = NB - 1 - jj if reverse else jj
            chunk = blk * NB + j
            fresh = _seq_start((chunk + 1) * C if reverse else chunk * C, *seq)
            rs = slice(j * C, (j + 1) * C)
            g = g_ref[j]
            for p in range(D_RWKV // PW):
                sl = slice(p * PW, (p + 1) * PW)
                s0 = jnp.where(fresh, 0.0, s_ref[d, p])
                pm = jnp.where(same_head, _tile_rows(p_ref[rs, sl], NP), 0.0)
                chains.append((d, p, rs, sl, s0, _dot(s0, pm), g[:, sl]))
        for d, p, rs, sl, s0, s0_p, g in chains:
            q_ref, y0_ref, _, h_ref, _, y_ref = dirs[d]
            y_ref[rs, sl] = _dot_nt(q_ref[rs, sl], s0) + y0_ref[rs, sl].astype(F32)
            hm = jnp.where(same_head, _tile_rows(h_ref[rs, sl].astype(F32), NP), 0.0)
            s_ref[d, p] = s0 * g + s0_p + hm


def _wkv_inter(qf, y0f, pf, hf, gf, qb, y0b, pb, hb, gb, seq):
    m = qf.shape[0]
    nb = WKV_INTER_CHUNKS
    tb = nb * WKV_CHUNK
    n_steps = m // tb
    fwd = pl.BlockSpec((tb, D_RWKV), lambda i: (i, 0))
    bwd = pl.BlockSpec((tb, D_RWKV), lambda i: (n_steps - 1 - i, 0))
    gfwd = pl.BlockSpec((nb, 1, D_RWKV), lambda i: (i, 0, 0))
    gbwd = pl.BlockSpec((nb, 1, D_RWKV), lambda i: (n_steps - 1 - i, 0, 0))
    out = jax.ShapeDtypeStruct((m, D_RWKV), F32)
    return pl.pallas_call(
        functools.partial(_wkv_inter_kernel, seq), grid=(n_steps,),
        in_specs=[fwd] * 4 + [gfwd] + [bwd] * 4 + [gbwd],
        out_specs=[fwd, bwd], out_shape=[out, out],
        scratch_shapes=[pltpu.VMEM((2, D_RWKV // WKV_PW, WKV_PW, WKV_PW), F32)],
        compiler_params=_cparams(), name="wkv_inter")(qf, y0f, pf, hf, gf, qb, y0b, pb, hb, gb)


def _post_kernel(after_ref, x_ref, yf_ref, yb_ref, bonus_ref, g_ref, conv_ref, gt_ref, hs_ref,
                 lnxg_ref, lnxb_ref, wrb_ref, wout_ref, bout_ref, ln1g_ref, ln1b_ref,
                 wrh_ref, wrl_ref, br_ref,
                 x1_ref, x1b_ref, tope_ref, gate_ref, rank_ref, count_ref, seen_ref):
    hs = hs_ref[...]
    y = yf_ref[...] + yb_ref[...]
    inv_n = 1.0 / RWKV_HEAD
    mu = _dot(y, hs) * inv_n
    yc = y - mu
    var = _dot(yc * yc, hs) * inv_n
    y = yc * lax.rsqrt(var + GN_EPS) * lnxg_ref[...] + lnxb_ref[...]
    y = (y + bonus_ref[...].astype(F32)) * g_ref[...].astype(F32)
    rwkv_out = _dot(y, wrb_ref[...])
    gt = gt_ref[...]
    mixed = (jax.nn.sigmoid(gt[:, :D_MODEL]) * conv_ref[...]
             + jax.nn.sigmoid(gt[:, D_MODEL:]) * rwkv_out)
    mix = _dot(mixed, wout_ref[...]) + bout_ref[...]
    x1 = _layer_norm(DEEPNORM_ALPHA * x_ref[...] + mix, ln1g_ref[...], ln1b_ref[...], LN_EPS)
    x1_ref[...] = x1
    x1b_ref[...] = x1.astype(BF16)

    xh, xl = _split_bf16(x1)
    logits = (jnp.dot(xh, wrh_ref[...], preferred_element_type=F32)
              + jnp.dot(xh, wrl_ref[...], preferred_element_type=F32)
              + jnp.dot(xl, wrh_ref[...], preferred_element_type=F32)) + br_ref[...]
    lane = lax.broadcasted_iota(jnp.int32, logits.shape, 1).astype(F32)
    neg = jnp.float32(-jnp.inf)
    logits = jnp.where(lane < N_EXPERTS, logits, neg)
    tops, idxs = [], []
    for _ in range(TOP_K):
        mx = jnp.max(logits, axis=-1, keepdims=True)
        ix = jnp.min(jnp.where(logits == mx, lane, float(LANES)), axis=-1, keepdims=True)
        tops.append(mx)
        idxs.append(ix)
        logits = jnp.where(lane == ix, neg, logits)
    es = [jnp.exp(t - tops[0]) for t in tops]
    denom = es[0] + es[1] + es[2] + es[3]
    i = pl.program_id(0)
    tm = logits.shape[0]

    @pl.when(i == 0)
    def _():
        seen_ref[...] = jnp.zeros_like(seen_ref)

    onehots = [(lane == ix).astype(F32) for ix in idxs]
    picked = onehots[0] + onehots[1] + onehots[2] + onehots[3]
    t_r = lax.broadcasted_iota(jnp.int32, (tm, tm), 0)
    t_c = lax.broadcasted_iota(jnp.int32, (tm, tm), 1)
    before = _dot((t_c < t_r).astype(BF16), picked) + seen_ref[...]
    tope = jnp.zeros(logits.shape, F32)
    gate = jnp.zeros(logits.shape, F32)
    rank = jnp.zeros(logits.shape, F32)
    for j in range(TOP_K):
        tope = jnp.where(lane == j, idxs[j], tope)
        gate = jnp.where(lane == j, es[j] / denom, gate)
        rank = jnp.where(lane == j, jnp.sum(onehots[j] * before, axis=-1, keepdims=True), rank)
    tope_ref[...] = tope.astype(jnp.int32)
    gate_ref[...] = gate
    rank_ref[...] = rank.astype(jnp.int32)
    seen = seen_ref[...] + jnp.sum(picked, axis=0, keepdims=True)
    seen_ref[...] = seen
    count_ref[...] = jnp.broadcast_to(seen, count_ref.shape).astype(jnp.int32)


def _post(x, yf, yb, bonus, g, conv, gt, hs, lnxg, lnxb, wrb, wout, bout, ln1g, ln1b,
          wrh, wrl, br, layer, group, after):
    tm = TOK_TILE
    m = x.shape[0] // MOE_GROUPS
    ft = group * (m // tm)
    return pl.pallas_call(
        _post_kernel, grid=(m // tm,),
        in_specs=[pl.BlockSpec(memory_space=pl.ANY)]
                 + [_row_spec(tm, D_MODEL, ft)] + [_row_spec(tm, D_RWKV, ft)] * 4
                 + [_row_spec(tm, D_MODEL, ft), _row_spec(tm, N_GATES, ft), _const_spec((D_RWKV, D_RWKV)),
                    _const_spec((1, D_RWKV), layer), _const_spec((1, D_RWKV), layer),
                    _const_spec((D_RWKV, D_MODEL), layer), _const_spec((D_MODEL, D_MODEL), layer),
                    _const_spec((1, D_MODEL), layer), _const_spec((1, D_MODEL), layer),
                    _const_spec((1, D_MODEL), layer),
                    _const_spec((D_MODEL, LANES), layer), _const_spec((D_MODEL, LANES), layer),
                    _const_spec((1, LANES), layer)],
        out_specs=[_row_spec(tm, D_MODEL), _row_spec(tm, D_MODEL), _row_spec(tm, LANES),
                   _row_spec(tm, LANES), _row_spec(tm, LANES),
                   pl.BlockSpec((SUBLANES, LANES), lambda i: (0, 0))],
        out_shape=[jax.ShapeDtypeStruct((m, D_MODEL), F32), jax.ShapeDtypeStruct((m, D_MODEL), BF16),
                   jax.ShapeDtypeStruct((m, LANES), jnp.int32), jax.ShapeDtypeStruct((m, LANES), F32),
                   jax.ShapeDtypeStruct((m, LANES), jnp.int32),
                   jax.ShapeDtypeStruct((SUBLANES, LANES), jnp.int32)],
        scratch_shapes=[pltpu.VMEM((1, LANES), F32)],
        compiler_params=_cparams(), name="post")(
            after, x, yf, yb, bonus, g, conv, gt, hs, lnxg, lnxb, wrb, wout, bout, ln1g, ln1b, wrh, wrl, br)


def _moe_kernel(blk_e_ref, nvalid_ref, x_ref, wgu_ref, bgu_ref, wd_ref, bd_ref, o_ref,
                wgu_bf_ref, wd_bf_ref):
    i = pl.program_id(0)
    valid = i < nvalid_ref[0]
    new_expert = (i == 0) | (blk_e_ref[i] != blk_e_ref[jnp.maximum(i - 1, 0)])

    @pl.when(valid & new_expert)
    def _():
        wgu_bf_ref[...] = wgu_ref[...].astype(BF16)
        wd_bf_ref[...] = wd_ref[...].astype(BF16)

    @pl.when(valid)
    def _():
        hgu = jnp.dot(x_ref[...], wgu_bf_ref[...], preferred_element_type=F32) + bgu_ref[...]
        hg = jnp.minimum(hgu[:, :D_EXPERT], SWIGLU_LIMIT)
        hl = jnp.clip(hgu[:, D_EXPERT:], -SWIGLU_LIMIT, SWIGLU_LIMIT)
        act = hg * jax.nn.sigmoid(SWIGLU_ALPHA * hg) * (hl + 1.0)
        o_ref[...] = (_dot(act, wd_bf_ref[...]) + bd_ref[...]).astype(o_ref.dtype)

    @pl.when(i >= nvalid_ref[0])
    def _():
        o_ref[...] = jnp.zeros_like(o_ref)


def _moe(blk_e, nvalid, xs, wgu, bgu, wd, bd, layer):
    n_rows = xs.shape[0]
    tm = MOE_TILE
    grid_spec = pltpu.PrefetchScalarGridSpec(
        num_scalar_prefetch=2, grid=(n_rows // tm,),
        in_specs=[pl.BlockSpec((tm, D_MODEL), lambda i, be, nv: (i, 0)),
                  pl.BlockSpec((None, None, D_MODEL, 2 * D_EXPERT), lambda i, be, nv: (layer, be[i], 0, 0)),
                  pl.BlockSpec((None, None, 1, 2 * D_EXPERT), lambda i, be, nv: (layer, be[i], 0, 0)),
                  pl.BlockSpec((None, None, D_EXPERT, D_MODEL), lambda i, be, nv: (layer, be[i], 0, 0)),
                  pl.BlockSpec((None, None, 1, D_MODEL), lambda i, be, nv: (layer, be[i], 0, 0))],
        out_specs=pl.BlockSpec((tm, D_MODEL), lambda i, be, nv: (i, 0)),
        scratch_shapes=[pltpu.VMEM((D_MODEL, 2 * D_EXPERT), BF16), pltpu.VMEM((D_EXPERT, D_MODEL), BF16)])
    return pl.pallas_call(
        _moe_kernel, grid_spec=grid_spec,
        out_shape=jax.ShapeDtypeStruct((n_rows, D_MODEL), BF16),
        compiler_params=pltpu.CompilerParams(dimension_semantics=("arbitrary",),
                                             vmem_limit_bytes=MOE_VMEM_LIMIT),
        name="moe")(blk_e, nvalid, xs, wgu, bgu, wd, bd)


def _route(top_e, rank_in_e, counts, tile):
    m = top_e.shape[0]
    n_assign = m * TOP_K
    flat_e = top_e.reshape(-1)
    iota = jnp.arange(n_assign, dtype=jnp.int32)
    experts = jnp.arange(N_EXPERTS, dtype=jnp.int32)
    _, order = lax.sort((flat_e, iota), num_keys=1, is_stable=True)
    is_e = flat_e[:, None] == experts[None, :]
    start = jnp.cumsum(counts) - counts
    padded = (counts + tile - 1) // tile * tile
    pend = jnp.cumsum(padded)
    shift = pend - padded - start
    pos = rank_in_e.reshape(-1) + jnp.sum(jnp.where(is_e, (pend - padded)[None, :], 0), axis=1)
    n_blocks = n_assign // tile + N_EXPERTS
    blk_first = jnp.arange(n_blocks, dtype=jnp.int32) * tile
    blk_e = jnp.minimum(jnp.sum(pend[None, :] <= blk_first[:, None], axis=1, dtype=jnp.int32),
                        N_EXPERTS - 1)
    src = (blk_first - shift[blk_e])[:, None] + jnp.arange(tile, dtype=jnp.int32)[None, :]
    valid = src < (start + counts)[blk_e][:, None]
    row_tok = jnp.where(valid, order[jnp.clip(src, 0, n_assign - 1)] // TOP_K, 0).reshape(-1)
    nvalid = (pend[-1:] // tile).astype(jnp.int32)
    return row_tok, pos.reshape(m, TOP_K), blk_e, nvalid


def _final_kernel(x1_ref, yg_ref, gate_ref, p_ref, wpg_ref, bpg_ref, wpp_ref, g_ref, b_ref, *rest):
    o_ref = rest[-1]
    x1 = x1_ref[...]
    gate = gate_ref[...]
    ffn = jnp.zeros(x1.shape, F32)
    for j in range(TOP_K):
        ffn = ffn + gate[:, j:j + 1] * yg_ref[j].astype(F32)
    ple = jax.nn.sigmoid(_dot(x1, wpg_ref[...]) + bpg_ref[...]) * _dot(p_ref[...], wpp_ref[...])
    o_ref[...] = _layer_norm(DEEPNORM_ALPHA * x1 + ffn + ple, g_ref[...], b_ref[...], LN_EPS)


def _final(x1, yg, gate, p, wpg, bpg, wpp, g, b, layer, group, out_prev):
    tm = TOK_TILE
    mg = x1.shape[0]
    ft = group * (mg // tm)
    in_specs = [_row_spec(tm, D_MODEL),
                pl.BlockSpec((TOP_K, tm, D_MODEL), lambda i: (0, i, 0)),
                _row_spec(tm, LANES),
                pl.BlockSpec((None, tm, D_PLE), lambda i: (layer, i + ft, 0)),
                _const_spec((D_MODEL, D_MODEL), layer), _const_spec((1, D_MODEL), layer),
                _const_spec((D_PLE, D_MODEL), layer), _const_spec((1, D_MODEL), layer),
                _const_spec((1, D_MODEL), layer)]
    args = [x1, yg, gate, p, wpg, bpg, wpp, g, b]
    aliases = {}
    if out_prev is not None:
        in_specs.append(pl.BlockSpec(memory_space=pl.ANY))
        args.append(out_prev)
        aliases = {len(args) - 1: 0}
    return pl.pallas_call(
        _final_kernel, grid=(mg // tm,),
        in_specs=in_specs,
        out_specs=_row_spec(tm, D_MODEL, ft),
        out_shape=jax.ShapeDtypeStruct((mg * MOE_GROUPS, D_MODEL), F32),
        input_output_aliases=aliases,
        compiler_params=_cparams(), name="final")(*args)


def _pad_lowrank(w):
    z = jnp.zeros_like(w[:, 0])
    fwd = jnp.concatenate([w[:, 0], z], axis=1)
    bwd = jnp.concatenate([z, w[:, 1]], axis=1)
    return jnp.stack([fwd, bwd], axis=1)


def kernel(x_prompt, x_sample, p_prompt, p_sample, ln0_g, ln0_b, w_in, b_in, mu_shift, dw_w, dw_b, cln_g, cln_b, w_cb, b_cb, w0, w2, a0, a2, g2, k_k, k_a, r_k, lnx_g, lnx_b, w_rb, w_out, b_out, ln1_g, ln1_b, w_router, b_router, w_gu, b_gu, w_down, b_down, w_pg, b_pg, w_pp, ln2_g, ln2_b):
    bp, tp, d = x_prompt.shape
    bs, ts, _ = x_sample.shape
    mp, ms = bp * tp, bs * ts
    m = mp + ms
    assert d == D_MODEL and tp % TOK_TILE == 0 and ts % TOK_TILE == 0
    assert m % LN0_TILE == 0 and m % (MOE_GROUPS * TOK_TILE) == 0
    assert (m // MOE_GROUPS * TOP_K) % MOE_TILE == 0
    assert m % (WKV_INTRA_CHUNKS * WKV_CHUNK) == 0 and m % (WKV_INTER_CHUNKS * WKV_CHUNK) == 0
    seq = (mp, tp, ts, m)
    L = w_in.shape[0]

    x = jnp.concatenate([x_prompt.reshape(mp, d), x_sample.reshape(ms, d)], axis=0)
    p = jnp.concatenate([p_prompt.reshape(L, mp, D_PLE), p_sample.reshape(L, ms, D_PLE)], axis=1)

    row = lambda a: a.reshape(a.shape[0], 1, -1)
    wc = w_in[:, :, :N_GLU].astype(BF16)
    wz = w_in[:, :, N_GLU:N_GLU + N_SHIFT].astype(BF16)
    wg = w_in[:, :, N_GLU + N_SHIFT:].astype(BF16)
    bc, bz, bg = row(b_in[:, :N_GLU]), row(b_in[:, N_GLU:N_GLU + N_SHIFT]), row(b_in[:, N_GLU + N_SHIFT:])
    w2p = _pad_lowrank(w2).astype(BF16)
    a2p = _pad_lowrank(a2).astype(BF16)
    head = jnp.arange(D_RWKV, dtype=jnp.int32) // RWKV_HEAD
    hs = (head[:, None] == head[None, :]).astype(BF16)
    wr_pad = jnp.pad(w_router, ((0, 0), (0, 0), (0, LANES - N_EXPERTS)))
    wrh = wr_pad.astype(BF16)
    wrl = (wr_pad - wrh.astype(F32)).astype(BF16)
    br = row(jnp.pad(b_router, ((0, 0), (0, LANES - N_EXPERTS))))
    bgu = b_gu.reshape(L, N_EXPERTS, 1, 2 * D_EXPERT)
    bdn = b_down.reshape(L, N_EXPERTS, 1, D_MODEL)
    wcb_b, wrb_b, wout_b = w_cb.astype(BF16), w_rb.astype(BF16), w_out.astype(BF16)
    wpg_b, wpp_b, g2_b = w_pg.astype(BF16), w_pp.astype(BF16), g2.astype(BF16)

    x = _ln0(x, ln0_g, ln0_b)
    for l in range(L):
        c, z, gt = _in_proj(x, wc, wz, wg, bc, bz, bg, l)
        conv = _conv_branch(c, dw_w, row(dw_b), row(cln_g), row(cln_b), wcb_b, row(b_cb), l, seq)
        (r, v, kk, g, bonus, lwf, kdf, bf, lwb, kdb, bb) = _rwkv_prep(
            z, mu_shift, w0, w2p, a0, a2p, g2_b, row(k_k), row(k_a), row(r_k.reshape(L, D_RWKV)), hs, l, seq)
        yf, yb = _wkv_inter(*_wkv_intra(r, v, kk, lwf, kdf, bf, lwb, kdb, bb), seq)
        x_next = None
        after = jnp.zeros((MOE_TILE,), jnp.int32)
        for h in range(MOE_GROUPS):
            x1, x1b, tope, gate, rank, counts = _post(
                x, yf, yb, bonus, g, conv, gt, hs, row(lnx_g), row(lnx_b), wrb_b, wout_b, row(b_out),
                row(ln1_g), row(ln1_b), wrh, wrl, br, l, h, after)
            row_tok, pos, blk_e, nvalid = _route(
                tope[:, :TOP_K], rank[:, :TOP_K], counts[0, :N_EXPERTS], MOE_TILE)
            after = row_tok[:MOE_TILE]
            y_rows = _moe(blk_e, nvalid, x1b[row_tok], w_gu, bgu, w_down, bdn, l)
            x_next = _final(x1, y_rows[pos.T], gate, p, wpg_b, row(b_pg), wpp_b, row(ln2_g),
                            row(ln2_b), l, h, x_next)
        x = x_next

    return (x[:mp].reshape(bp, tp, d), x[mp:].reshape(bs, ts, d))
```

```python
import functools
import math

import jax
import jax.numpy as jnp
from jax import lax
from jax.experimental import pallas as pl
from jax.experimental.pallas import tpu as pltpu

F32 = jnp.float32
BF16 = jnp.bfloat16

D_MODEL = 1024
DEPTH = 4
D_CONV = 512
CONV_WIDTH = 31
CONV_HALO = 16
D_RWKV = 512
RWKV_HEAD = 64
DECAY_RANK = 64
ICLR_RANK = 64
GATE_RANK = 128
N_EXPERTS = 32
TOP_K = 4
D_EXPERT = 1024
SWIGLU_LIMIT = 7.0
SWIGLU_ALPHA = 1.702
D_PLE = 256
LN_EPS = 1e-5
GN_EPS = 64e-5
DEEPNORM_ALPHA = (2 * DEPTH) ** 0.25
DECAY_SCALE = math.exp(-0.5)
N_GLU = 2 * D_CONV
N_SHIFT = 3 * D_RWKV + 2 * DECAY_RANK + 2 * ICLR_RANK + GATE_RANK
N_GATES = 2 * D_MODEL
N_LOWRANK = 2 * DECAY_RANK

LANES = 128
SUBLANES = 8
SHIFT_HALO = SUBLANES
TOK_TILE = 256
LN0_TILE = 512
MOE_TILE = 512
MOE_GROUPS = 2
WKV_CHUNK = 64
WKV_PAIR = 2
WKV_PW = WKV_PAIR * RWKV_HEAD
WKV_INTER_CHUNKS = 4
WKV_INTRA_CHUNKS = 4
VMEM_LIMIT = 48 * 1024 * 1024
MOE_VMEM_LIMIT = 58 * 1024 * 1024
assert WKV_CHUNK == RWKV_HEAD


def _cparams(n_axes=1):
    return pltpu.CompilerParams(dimension_semantics=("arbitrary",) * n_axes,
                                vmem_limit_bytes=VMEM_LIMIT)


def _dot(a, b):
    return jnp.dot(a.astype(BF16), b.astype(BF16), preferred_element_type=F32)


def _dot_nt(a, b):
    return lax.dot_general(a.astype(BF16), b.astype(BF16), (((1,), (1,)), ((), ())),
                           preferred_element_type=F32)


def _dot_tn(a, b):
    return lax.dot_general(a.astype(BF16), b.astype(BF16), (((0,), (0,)), ((), ())),
                           preferred_element_type=F32)


def _split_bf16(x):
    hi = x.astype(BF16)
    lo = (x - hi.astype(F32)).astype(BF16)
    return hi, lo


def _layer_norm(x, g, b, eps):
    mu = jnp.mean(x, axis=-1, keepdims=True)
    xc = x - mu
    var = jnp.mean(xc * xc, axis=-1, keepdims=True)
    return xc * lax.rsqrt(var + eps) * g + b


def _seq_start(tok, m_prompt, t_prompt, t_sample, m_total):
    in_prompt = tok < m_prompt
    at_prompt_start = lax.rem(tok, jnp.int32(t_prompt)) == 0
    at_sample_start = lax.rem(jnp.abs(tok - m_prompt), jnp.int32(t_sample)) == 0
    return ((in_prompt & at_prompt_start) | (jnp.logical_not(in_prompt) & at_sample_start)
            | (tok >= m_total))


def _const_spec(shape, layer=None):
    if layer is None:
        nd = len(shape)
        return pl.BlockSpec(shape, lambda i, _n=nd: (0,) * _n)
    nd = len(shape)
    return pl.BlockSpec((None,) + tuple(shape), lambda i, _l=layer, _n=nd: (_l,) + (0,) * _n)


def _row_spec(tm, width, first_tile=0):
    return pl.BlockSpec((tm, width), lambda i: (i + first_tile, 0))


def _ln_kernel(x_ref, g_ref, b_ref, o_ref):
    o_ref[...] = _layer_norm(x_ref[...], g_ref[...], b_ref[...], LN_EPS)


def _ln0(x, g, b):
    m = x.shape[0]
    tm = LN0_TILE
    return pl.pallas_call(
        _ln_kernel, grid=(m // tm,),
        in_specs=[_row_spec(tm, D_MODEL), _const_spec((1, D_MODEL)), _const_spec((1, D_MODEL))],
        out_specs=_row_spec(tm, D_MODEL),
        out_shape=jax.ShapeDtypeStruct((m, D_MODEL), F32),
        compiler_params=_cparams(), name="ln0")(x, g.reshape(1, -1), b.reshape(1, -1))


def _in_proj_kernel(x_ref, wc_ref, wz_ref, wg_ref, bc_ref, bz_ref, bg_ref, c_ref, z_ref, gt_ref):
    xb = x_ref[...].astype(BF16)
    c_ref[...] = jnp.dot(xb, wc_ref[...], preferred_element_type=F32) + bc_ref[...]
    z_ref[...] = jnp.dot(xb, wz_ref[...], preferred_element_type=F32) + bz_ref[...]
    gt_ref[...] = jnp.dot(xb, wg_ref[...], preferred_element_type=F32) + bg_ref[...]


def _in_proj(x, wc, wz, wg, bc, bz, bg, layer):
    m = x.shape[0]
    tm = TOK_TILE
    return pl.pallas_call(
        _in_proj_kernel, grid=(m // tm,),
        in_specs=[_row_spec(tm, D_MODEL),
                  _const_spec((D_MODEL, N_GLU), layer), _const_spec((D_MODEL, N_SHIFT), layer),
                  _const_spec((D_MODEL, N_GATES), layer),
                  _const_spec((1, N_GLU), layer), _const_spec((1, N_SHIFT), layer),
                  _const_spec((1, N_GATES), layer)],
        out_specs=[_row_spec(tm, N_GLU), _row_spec(tm, N_SHIFT), _row_spec(tm, N_GATES)],
        out_shape=[jax.ShapeDtypeStruct((m, N_GLU), F32), jax.ShapeDtypeStruct((m, N_SHIFT), F32),
                   jax.ShapeDtypeStruct((m, N_GATES), F32)],
        compiler_params=_cparams(), name="in_proj")(x, wc, wz, wg, bc, bz, bg)


def _conv_kernel(seq, c_ref, cp_ref, cn_ref, dww_ref, dwb_ref, g_ref, b_ref, wcb_ref, bcb_ref,
                 o_ref, u_ref, sh_ref):
    tm = c_ref.shape[0]
    i = pl.program_id(0)
    first = _seq_start(i * tm, *seq)
    last = _seq_start((i + 1) * tm, *seq)

    def glu(c):
        return c[:, :D_CONV] * jax.nn.sigmoid(c[:, D_CONV:])

    u_ref[0:CONV_HALO, :] = jnp.where(first, 0.0, glu(cp_ref[...]))
    u_ref[CONV_HALO:CONV_HALO + tm, :] = glu(c_ref[...])
    u_ref[CONV_HALO + tm:, :] = jnp.where(last, 0.0, glu(cn_ref[...]))

    span = sh_ref.shape[1]
    for s in range(SUBLANES):
        sh_ref[s] = u_ref[s:s + span, :]
    base = CONV_HALO - CONV_WIDTH // 2
    acc = jnp.zeros((tm, D_CONV), F32) + dwb_ref[...]
    for j in range(CONV_WIDTH):
        a, s = divmod(base + j, SUBLANES)
        acc = acc + sh_ref[s, a * SUBLANES:a * SUBLANES + tm, :] * dww_ref[j:j + 1, :]
    u = _layer_norm(acc, g_ref[...], b_ref[...], LN_EPS)
    u = u * jax.nn.sigmoid(u)
    o_ref[...] = _dot(u, wcb_ref[...]) + bcb_ref[...]


def _conv_branch(c, dww, dwb, g, b, wcb, bcb, layer, seq):
    m = c.shape[0]
    tm = TOK_TILE
    r = tm // CONV_HALO
    nh = m // CONV_HALO
    return pl.pallas_call(
        functools.partial(_conv_kernel, seq), grid=(m // tm,),
        in_specs=[_row_spec(tm, N_GLU),
                  pl.BlockSpec((CONV_HALO, N_GLU), lambda i: (jnp.maximum(i * r - 1, 0), 0)),
                  pl.BlockSpec((CONV_HALO, N_GLU), lambda i: (jnp.minimum((i + 1) * r, nh - 1), 0)),
                  _const_spec((CONV_WIDTH, D_CONV), layer), _const_spec((1, D_CONV), layer),
                  _const_spec((1, D_CONV), layer), _const_spec((1, D_CONV), layer),
                  _const_spec((D_CONV, D_MODEL), layer), _const_spec((1, D_MODEL), layer)],
        out_specs=_row_spec(tm, D_MODEL),
        out_shape=jax.ShapeDtypeStruct((m, D_MODEL), F32),
        scratch_shapes=[pltpu.VMEM((tm + 2 * CONV_HALO, D_CONV), F32),
                        pltpu.VMEM((SUBLANES, tm + 2 * CONV_HALO - SUBLANES, D_CONV), F32)],
        compiler_params=_cparams(), name="conv_branch")(c, c, c, dww, dwb, g, b, wcb, bcb)


def _rwkv_prep_kernel(seq, z_ref, zp_ref, zn_ref, mu_ref, w0_ref, w2_ref, a0_ref, a2_ref, g2_ref,
                      kk_ref, ka_ref, rk_ref, hs_ref,
                      r_ref, v_ref, kkn_ref, g_ref, bonus_ref,
                      lwf_ref, kdf_ref, bf_ref, lwb_ref, kdb_ref, bb_ref):
    tm = z_ref.shape[0]
    i = pl.program_id(0)
    first = _seq_start(i * tm, *seq)
    last = _seq_start((i + 1) * tm, *seq)

    z = z_ref[...]
    row = lax.broadcasted_iota(jnp.int32, (tm, 1), 0)
    prev_row = jnp.where(first, 0.0, zp_ref[SHIFT_HALO - 1:SHIFT_HALO, :])
    next_row = jnp.where(last, 0.0, zn_ref[0:1, :])
    zp = jnp.where(row == 0, prev_row, pltpu.roll(z, 1, 0))
    zn = jnp.where(row == tm - 1, next_row, pltpu.roll(z, tm - 1, 0))
    zs = z + mu_ref[0:1, :] * (zp - z) + mu_ref[1:2, :] * (zn - z)

    o = 3 * D_RWKV
    r = zs[:, 0:D_RWKV]
    k = zs[:, D_RWKV:2 * D_RWKV]
    v = zs[:, 2 * D_RWKV:o]
    lw = zs[:, o:o + N_LOWRANK]
    la = zs[:, o + N_LOWRANK:o + 2 * N_LOWRANK]
    lg = zs[:, o + 2 * N_LOWRANK:]

    hs = hs_ref[...]
    kk = k * kk_ref[...]
    kk = kk * lax.rsqrt(_dot(kk * kk, hs) + 1e-12)
    r_ref[...] = r.astype(BF16)
    v_ref[...] = v.astype(BF16)
    kkn_ref[...] = kk.astype(BF16)
    g_ref[...] = _dot(jax.nn.sigmoid(lg), g2_ref[...]).astype(BF16)
    bonus_ref[...] = (_dot(r * k * rk_ref[...], hs) * v).astype(BF16)

    tlw = jnp.tanh(lw)
    ka = ka_ref[...]
    for d, (lw_o, kd_o, b_o) in enumerate(((lwf_ref, kdf_ref, bf_ref), (lwb_ref, kdb_ref, bb_ref))):
        t = w0_ref[d:d + 1, :] + _dot(tlw, w2_ref[d])
        a = jax.nn.sigmoid(a0_ref[d:d + 1, :] + _dot(la, a2_ref[d]))
        lw_o[...] = -DECAY_SCALE * jax.nn.sigmoid(t)
        kd_o[...] = (k * (1.0 + (a - 1.0) * ka)).astype(BF16)
        b_o[...] = (kk * a).astype(BF16)


def _rwkv_prep(z, mu, w0, w2p, a0, a2p, g2, k_k, k_a, r_k, hs, layer, seq):
    m = z.shape[0]
    tm = TOK_TILE
    r = tm // SHIFT_HALO
    nh = m // SHIFT_HALO
    f32_tok = jax.ShapeDtypeStruct((m, D_RWKV), F32)
    bf_tok = jax.ShapeDtypeStruct((m, D_RWKV), BF16)
    outs = [bf_tok] * 5 + [f32_tok, bf_tok, bf_tok] * 2
    return pl.pallas_call(
        functools.partial(_rwkv_prep_kernel, seq), grid=(m // tm,),
        in_specs=[_row_spec(tm, N_SHIFT),
                  pl.BlockSpec((SHIFT_HALO, N_SHIFT), lambda i: (jnp.maximum(i * r - 1, 0), 0)),
                  pl.BlockSpec((SHIFT_HALO, N_SHIFT), lambda i: (jnp.minimum((i + 1) * r, nh - 1), 0)),
                  _const_spec((2, N_SHIFT), layer), _const_spec((2, D_RWKV), layer),
                  _const_spec((2, N_LOWRANK, D_RWKV), layer), _const_spec((2, D_RWKV), layer),
                  _const_spec((2, N_LOWRANK, D_RWKV), layer), _const_spec((GATE_RANK, D_RWKV), layer),
                  _const_spec((1, D_RWKV), layer), _const_spec((1, D_RWKV), layer),
                  _const_spec((1, D_RWKV), layer), _const_spec((D_RWKV, D_RWKV))],
        out_specs=[_row_spec(tm, D_RWKV)] * 11,
        out_shape=outs,
        compiler_params=_cparams(), name="rwkv_prep")(z, z, z, mu, w0, w2p, a0, a2p, g2, k_k, k_a, r_k, hs)


def _tile_rows(x, n):
    return jnp.concatenate([x] * n, axis=0)


def _fold_rows(x, n):
    h = x.shape[0] // n
    acc = x[0:h]
    for j in range(1, n):
        acc = acc + x[j * h:(j + 1) * h]
    return acc


def _same_head_mask():
    rows = lax.broadcasted_iota(jnp.int32, (WKV_PW, WKV_PW), 0)
    cols = lax.broadcasted_iota(jnp.int32, (WKV_PW, WKV_PW), 1)
    head_shift = RWKV_HEAD.bit_length() - 1
    same_head = (rows >> head_shift) == (cols >> head_shift)
    return same_head, rows & (RWKV_HEAD - 1), cols & (RWKV_HEAD - 1), rows == cols


def _wkv_intra_kernel(r_ref, v_ref, kk_ref, lwf_ref, kdf_ref, bf_ref, lwb_ref, kdb_ref, bb_ref,
                      qf_ref, y0f_ref, pf_ref, hf_ref, gf_ref, qb_ref, y0b_ref, pb_ref, hb_ref, gb_ref):
    C, PW, NP = WKV_CHUNK, WKV_PW, WKV_PAIR
    same_head, t_i, s_i, diag = _same_head_mask()
    eye = diag.astype(F32)
    t_c = lax.broadcasted_iota(jnp.int32, (C, C), 0)
    s_c = lax.broadcasted_iota(jnp.int32, (C, C), 1)

    def bd(x):
        return jnp.where(same_head, _tile_rows(x, NP), jnp.zeros((), x.dtype))

    def one_chunk(ci, carry):
        rows = pl.ds(pl.multiple_of(ci * C, C), C)
        r, v, kk = r_ref[rows, :].astype(F32), v_ref[rows, :], kk_ref[rows, :].astype(F32)
        probs = []
        for reverse, (lw_ref, kd_ref, b_ref), out_refs in (
                (False, (lwf_ref, kdf_ref, bf_ref), (qf_ref, y0f_ref, pf_ref, hf_ref, gf_ref)),
                (True, (lwb_ref, kdb_ref, bb_ref), (qb_ref, y0b_ref, pb_ref, hb_ref, gb_ref))):
            cum_mat = ((s_c >= t_c) if reverse else (s_c <= t_c)).astype(BF16)
            earlier = same_head & ((s_i > t_i) if reverse else (s_i < t_i))
            readable = earlier if reverse else (same_head & (s_i <= t_i))
            last_row = 0 if reverse else C - 1

            lw = lw_ref[rows, :]
            lw_hi, lw_lo = _split_bf16(lw)
            cum = (jnp.dot(cum_mat, lw_hi, preferred_element_type=F32)
                   + jnp.dot(cum_mat, lw_lo, preferred_element_type=F32))
            g_ex = jnp.exp(cum - lw)
            g_inv = jnp.exp(-cum)
            g_end = jnp.exp(cum[last_row:last_row + 1, :])
            al_all = -kk * g_ex
            be_all = b_ref[rows, :].astype(F32) * g_inv
            kb_all = kd_ref[rows, :].astype(F32) * g_inv
            rb_all = r * (g_ex if reverse else jnp.exp(cum))
            bt_all = be_all * g_end
            kt_all = kb_all * g_end
            out_refs[4][ci] = g_end
            for p in range(D_RWKV // PW):
                sl = slice(p * PW, (p + 1) * PW)
                probs.append(dict(sl=sl, out=out_refs, earlier=earlier, readable=readable,
                                  al=al_all[:, sl], be=be_all[:, sl], kb=kb_all[:, sl],
                                  rb=rb_all[:, sl], v=v[:, sl], bt=bt_all[:, sl], kt=kt_all[:, sl]))
        _wkv_intra_solve(probs, bd, eye, same_head, rows)
        return carry

    lax.fori_loop(0, WKV_INTRA_CHUNKS, one_chunk, 0)


def _wkv_intra_solve(probs, bd, eye, same_head, rows):
    C, PW, NP = WKV_CHUNK, WKV_PW, WKV_PAIR
    for c in probs:
        c["al_bd"] = bd(c["al"])
        aa = _dot_nt(jnp.concatenate([c["al_bd"], bd(c["rb"])], axis=0),
                     jnp.concatenate([_tile_rows(c["be"], NP), _tile_rows(c["kb"], NP)], axis=0))
        c["a_ab"] = jnp.where(c["earlier"], aa[:PW, :PW], 0.0)
        a_ak = jnp.where(c["earlier"], aa[:PW, PW:], 0.0)
        c["a_rb"] = jnp.where(c["readable"], aa[PW:, :PW], 0.0)
        a_rk = jnp.where(c["readable"], aa[PW:, PW:], 0.0)
        c["a_kk"] = jnp.concatenate([a_ak, a_rk], axis=0)
    for c in probs:
        c["akv"] = _dot(c["a_kk"], bd(c["v"]))
    for c in probs:
        c["inv"] = eye + c["a_ab"]
        c["pk"] = _dot(c["a_ab"], c["a_ab"])
    for _ in range(C.bit_length() - 3):
        for c in probs:
            sq_and_apply = _dot(c["pk"], jnp.concatenate([c["pk"], c["inv"]], axis=1))
            c["inv"] = c["inv"] + sq_and_apply[:, PW:]
            c["pk"] = sq_and_apply[:, :PW]
    for c in probs:
        c["inv"] = c["inv"] + _dot(c["pk"], c["inv"])
    for c in probs:
        c["wu"] = _dot(c["inv"], jnp.concatenate([c["al_bd"], c["akv"][:PW]], axis=1))
    for c in probs:
        c["qy"] = _dot(c["a_rb"], c["wu"])
    for c in probs:
        q_ref, y0_ref, p_ref, h_ref, _ = c["out"]
        sl, bt = c["sl"], c["bt"]
        q_ref[rows, sl] = (c["rb"] + _fold_rows(c["qy"][:, :PW], NP)).astype(BF16)
        y0_ref[rows, sl] = _fold_rows(c["qy"][:, PW:] + c["akv"][PW:], NP).astype(BF16)
        w = _fold_rows(c["wu"][:, :PW], NP)
        u0 = _fold_rows(c["wu"][:, PW:], NP)
        pm = _dot_tn(w, bt)
        hm = _dot_tn(jnp.concatenate([u0.astype(BF16), c["v"]], axis=0),
                     jnp.concatenate([bt, c["kt"]], axis=0))
        p_ref[rows, sl] = _fold_rows(jnp.where(same_head, pm, 0.0), NP).astype(BF16)
        h_ref[rows, sl] = _fold_rows(jnp.where(same_head, hm, 0.0), NP).astype(BF16)


def _wkv_intra(r, v, kk, lwf, kdf, bf, lwb, kdb, bb):
    m = r.shape[0]
    nb = WKV_INTRA_CHUNKS
    n_chunks = m // WKV_CHUNK
    spec = pl.BlockSpec((nb * WKV_CHUNK, D_RWKV), lambda i: (i, 0))
    gspec = pl.BlockSpec((nb, 1, D_RWKV), lambda i: (i, 0, 0))
    f32_tok = jax.ShapeDtypeStruct((m, D_RWKV), F32)
    bf_tok = jax.ShapeDtypeStruct((m, D_RWKV), BF16)
    g_shape = jax.ShapeDtypeStruct((n_chunks, 1, D_RWKV), F32)
    return pl.pallas_call(
        _wkv_intra_kernel, grid=(n_chunks // nb,),
        in_specs=[spec] * 9,
        out_specs=[spec, spec, spec, spec, gspec] * 2,
        out_shape=[bf_tok, bf_tok, bf_tok, bf_tok, g_shape] * 2,
        compiler_params=_cparams(), name="wkv_intra")(r, v, kk, lwf, kdf, bf, lwb, kdb, bb)


def _wkv_inter_kernel(seq, qf_ref, y0f_ref, pf_ref, hf_ref, gf_ref, qb_ref, y0b_ref, pb_ref, hb_ref,
                      gb_ref, yf_ref, yb_ref, s_ref):
    C, PW, NP, NB = WKV_CHUNK, WKV_PW, WKV_PAIR, WKV_INTER_CHUNKS
    n_steps = pl.num_programs(0)
    i = pl.program_id(0)
    same_head = _same_head_mask()[0]

    @pl.when(i == 0)
    def _():
        s_ref[...] = jnp.zeros_like(s_ref)

    dirs = ((qf_ref, y0f_ref, pf_ref, hf_ref, gf_ref, yf_ref),
            (qb_ref, y0b_ref, pb_ref, hb_ref, gb_ref, yb_ref))
    for jj in range(NB):
        chains = []
        for d, (q_ref, y0_ref, p_ref, h_ref, g_ref, y_ref) in enumerate(dirs):
            reverse = d == 1
            blk = (n_steps - 1 - i) if reverse else i
            j = NB - 1 - jj if reverse else jj
            chunk = blk * NB + j
            fresh = _seq_start((chunk + 1) * C if reverse else chunk * C, *seq)
            rs = slice(j * C, (j + 1) * C)
            g = g_ref[j]
            for p in range(D_RWKV // PW):
                sl = slice(p * PW, (p + 1) * PW)
                s0 = jnp.where(fresh, 0.0, s_ref[d, p])
                pm = jnp.where(same_head, _tile_rows(p_ref[rs, sl], NP), 0.0)
                chains.append((d, p, rs, sl, s0, _dot(s0, pm), g[:, sl]))
        for d, p, rs, sl, s0, s0_p, g in chains:
            q_ref, y0_ref, _, h_ref, _, y_ref = dirs[d]
            y_ref[rs, sl] = _dot_nt(q_ref[rs, sl], s0) + y0_ref[rs, sl].astype(F32)
            hm = jnp.where(same_head, _tile_rows(h_ref[rs, sl].astype(F32), NP), 0.0)
            s_ref[d, p] = s0 * g + s0_p + hm


def _wkv_inter(qf, y0f, pf, hf, gf, qb, y0b, pb, hb, gb, seq):
    m = qf.shape[0]
    nb = WKV_INTER_CHUNKS
    tb = nb * WKV_CHUNK
    n_steps = m // tb
    fwd = pl.BlockSpec((tb, D_RWKV), lambda i: (i, 0))
    bwd = pl.BlockSpec((tb, D_RWKV), lambda i: (n_steps - 1 - i, 0))
    gfwd = pl.BlockSpec((nb, 1, D_RWKV), lambda i: (i, 0, 0))
    gbwd = pl.BlockSpec((nb, 1, D_RWKV), lambda i: (n_steps - 1 - i, 0, 0))
    out = jax.ShapeDtypeStruct((m, D_RWKV), F32)
    return pl.pallas_call(
        functools.partial(_wkv_inter_kernel, seq), grid=(n_steps,),
        in_specs=[fwd] * 4 + [gfwd] + [bwd] * 4 + [gbwd],
        out_specs=[fwd, bwd], out_shape=[out, out],
        scratch_shapes=[pltpu.VMEM((2, D_RWKV // WKV_PW, WKV_PW, WKV_PW), F32)],
        compiler_params=_cparams(), name="wkv_inter")(qf, y0f, pf, hf, gf, qb, y0b, pb, hb, gb)


def _post_kernel(after_ref, x_ref, yf_ref, yb_ref, bonus_ref, g_ref, conv_ref, gt_ref, hs_ref,
                 lnxg_ref, lnxb_ref, wrb_ref, wout_ref, bout_ref, ln1g_ref, ln1b_ref,
                 wrh_ref, wrl_ref, br_ref,
                 x1_ref, x1b_ref, tope_ref, gate_ref, rank_ref, count_ref, seen_ref):
    hs = hs_ref[...]
    y = yf_ref[...] + yb_ref[...]
    inv_n = 1.0 / RWKV_HEAD
    mu = _dot(y, hs) * inv_n
    yc = y - mu
    var = _dot(yc * yc, hs) * inv_n
    y = yc * lax.rsqrt(var + GN_EPS) * lnxg_ref[...] + lnxb_ref[...]
    y = (y + bonus_ref[...].astype(F32)) * g_ref[...].astype(F32)
    rwkv_out = _dot(y, wrb_ref[...])
    gt = gt_ref[...]
    mixed = (jax.nn.sigmoid(gt[:, :D_MODEL]) * conv_ref[...]
             + jax.nn.sigmoid(gt[:, D_MODEL:]) * rwkv_out)
    mix = _dot(mixed, wout_ref[...]) + bout_ref[...]
    x1 = _layer_norm(DEEPNORM_ALPHA * x_ref[...] + mix, ln1g_ref[...], ln1b_ref[...], LN_EPS)
    x1_ref[...] = x1
    x1b_ref[...] = x1.astype(BF16)

    xh, xl = _split_bf16(x1)
    logits = (jnp.dot(xh, wrh_ref[...], preferred_element_type=F32)
              + jnp.dot(xh, wrl_ref[...], preferred_element_type=F32)
              + jnp.dot(xl, wrh_ref[...], preferred_element_type=F32)) + br_ref[...]
    lane = lax.broadcasted_iota(jnp.int32, logits.shape, 1).astype(F32)
    neg = jnp.float32(-jnp.inf)
    logits = jnp.where(lane < N_EXPERTS, logits, neg)
    tops, idxs = [], []
    for _ in range(TOP_K):
        mx = jnp.max(logits, axis=-1, keepdims=True)
        ix = jnp.min(jnp.where(logits == mx, lane, float(LANES)), axis=-1, keepdims=True)
        tops.append(mx)
        idxs.append(ix)
        logits = jnp.where(lane == ix, neg, logits)
    es = [jnp.exp(t - tops[0]) for t in tops]
    denom = es[0] + es[1] + es[2] + es[3]
    i = pl.program_id(0)
    tm = logits.shape[0]

    @pl.when(i == 0)
    def _():
        seen_ref[...] = jnp.zeros_like(seen_ref)

    onehots = [(lane == ix).astype(F32) for ix in idxs]
    picked = onehots[0] + onehots[1] + onehots[2] + onehots[3]
    t_r = lax.broadcasted_iota(jnp.int32, (tm, tm), 0)
    t_c = lax.broadcasted_iota(jnp.int32, (tm, tm), 1)
    before = _dot((t_c < t_r).astype(BF16), picked) + seen_ref[...]
    tope = jnp.zeros(logits.shape, F32)
    gate = jnp.zeros(logits.shape, F32)
    rank = jnp.zeros(logits.shape, F32)
    for j in range(TOP_K):
        tope = jnp.where(lane == j, idxs[j], tope)
        gate = jnp.where(lane == j, es[j] / denom, gate)
        rank = jnp.where(lane == j, jnp.sum(onehots[j] * before, axis=-1, keepdims=True), rank)
    tope_ref[...] = tope.astype(jnp.int32)
    gate_ref[...] = gate
    rank_ref[...] = rank.astype(jnp.int32)
    seen = seen_ref[...] + jnp.sum(picked, axis=0, keepdims=True)
    seen_ref[...] = seen
    count_ref[...] = jnp.broadcast_to(seen, count_ref.shape).astype(jnp.int32)


def _post(x, yf, yb, bonus, g, conv, gt, hs, lnxg, lnxb, wrb, wout, bout, ln1g, ln1b,
          wrh, wrl, br, layer, group, after):
    tm = TOK_TILE
    m = x.shape[0] // MOE_GROUPS
    ft = group * (m // tm)
    return pl.pallas_call(
        _post_kernel, grid=(m // tm,),
        in_specs=[pl.BlockSpec(memory_space=pl.ANY)]
                 + [_row_spec(tm, D_MODEL, ft)] + [_row_spec(tm, D_RWKV, ft)] * 4
                 + [_row_spec(tm, D_MODEL, ft), _row_spec(tm, N_GATES, ft), _const_spec((D_RWKV, D_RWKV)),
                    _const_spec((1, D_RWKV), layer), _const_spec((1, D_RWKV), layer),
                    _const_spec((D_RWKV, D_MODEL), layer), _const_spec((D_MODEL, D_MODEL), layer),
                    _const_spec((1, D_MODEL), layer), _const_spec((1, D_MODEL), layer),
                    _const_spec((1, D_MODEL), layer),
                    _const_spec((D_MODEL, LANES), layer), _const_spec((D_MODEL, LANES), layer),
                    _const_spec((1, LANES), layer)],
        out_specs=[_row_spec(tm, D_MODEL), _row_spec(tm, D_MODEL), _row_spec(tm, LANES),
                   _row_spec(tm, LANES), _row_spec(tm, LANES),
                   pl.BlockSpec((SUBLANES, LANES), lambda i: (0, 0))],
        out_shape=[jax.ShapeDtypeStruct((m, D_MODEL), F32), jax.ShapeDtypeStruct((m, D_MODEL), BF16),
                   jax.ShapeDtypeStruct((m, LANES), jnp.int32), jax.ShapeDtypeStruct((m, LANES), F32),
                   jax.ShapeDtypeStruct((m, LANES), jnp.int32),
                   jax.ShapeDtypeStruct((SUBLANES, LANES), jnp.int32)],
        scratch_shapes=[pltpu.VMEM((1, LANES), F32)],
        compiler_params=_cparams(), name="post")(
            after, x, yf, yb, bonus, g, conv, gt, hs, lnxg, lnxb, wrb, wout, bout, ln1g, ln1b, wrh, wrl, br)


def _moe_kernel(blk_e_ref, nvalid_ref, x_ref, wgu_ref, bgu_ref, wd_ref, bd_ref, o_ref,
                wgu_bf_ref, wd_bf_ref):
    i = pl.program_id(0)
    valid = i < nvalid_ref[0]
    new_expert = (i == 0) | (blk_e_ref[i] != blk_e_ref[jnp.maximum(i - 1, 0)])

    @pl.when(valid & new_expert)
    def _():
        wgu_bf_ref[...] = wgu_ref[...].astype(BF16)
        wd_bf_ref[...] = wd_ref[...].astype(BF16)

    @pl.when(valid)
    def _():
        hgu = jnp.dot(x_ref[...], wgu_bf_ref[...], preferred_element_type=F32) + bgu_ref[...]
        hg = jnp.minimum(hgu[:, :D_EXPERT], SWIGLU_LIMIT)
        hl = jnp.clip(hgu[:, D_EXPERT:], -SWIGLU_LIMIT, SWIGLU_LIMIT)
        act = hg * jax.nn.sigmoid(SWIGLU_ALPHA * hg) * (hl + 1.0)
        o_ref[...] = (_dot(act, wd_bf_ref[...]) + bd_ref[...]).astype(o_ref.dtype)

    @pl.when(i >= nvalid_ref[0])
    def _():
        o_ref[...] = jnp.zeros_like(o_ref)


def _moe(blk_e, nvalid, xs, wgu, bgu, wd, bd, layer):
    n_rows = xs.shape[0]
    tm = MOE_TILE
    grid_spec = pltpu.PrefetchScalarGridSpec(
        num_scalar_prefetch=2, grid=(n_rows // tm,),
        in_specs=[pl.BlockSpec((tm, D_MODEL), lambda i, be, nv: (i, 0)),
                  pl.BlockSpec((None, None, D_MODEL, 2 * D_EXPERT), lambda i, be, nv: (layer, be[i], 0, 0)),
                  pl.BlockSpec((None, None, 1, 2 * D_EXPERT), lambda i, be, nv: (layer, be[i], 0, 0)),
                  pl.BlockSpec((None, None, D_EXPERT, D_MODEL), lambda i, be, nv: (layer, be[i], 0, 0)),
                  pl.BlockSpec((None, None, 1, D_MODEL), lambda i, be, nv: (layer, be[i], 0, 0))],
        out_specs=pl.BlockSpec((tm, D_MODEL), lambda i, be, nv: (i, 0)),
        scratch_shapes=[pltpu.VMEM((D_MODEL, 2 * D_EXPERT), BF16), pltpu.VMEM((D_EXPERT, D_MODEL), BF16)])
    return pl.pallas_call(
        _moe_kernel, grid_spec=grid_spec,
        out_shape=jax.ShapeDtypeStruct((n_rows, D_MODEL), BF16),
        compiler_params=pltpu.CompilerParams(dimension_semantics=("arbitrary",),
                                             vmem_limit_bytes=MOE_VMEM_LIMIT),
        name="moe")(blk_e, nvalid, xs, wgu, bgu, wd, bd)


def _route(top_e, rank_in_e, counts, tile):
    m = top_e.shape[0]
    n_assign = m * TOP_K
    flat_e = top_e.reshape(-1)
    iota = jnp.arange(n_assign, dtype=jnp.int32)
    experts = jnp.arange(N_EXPERTS, dtype=jnp.int32)
    _, order = lax.sort((flat_e, iota), num_keys=1, is_stable=True)
    is_e = flat_e[:, None] == experts[None, :]
    start = jnp.cumsum(counts) - counts
    padded = (counts + tile - 1) // tile * tile
    pend = jnp.cumsum(padded)
    shift = pend - padded - start
    pos = rank_in_e.reshape(-1) + jnp.sum(jnp.where(is_e, (pend - padded)[None, :], 0), axis=1)
    n_blocks = n_assign // tile + N_EXPERTS
    blk_first = jnp.arange(n_blocks, dtype=jnp.int32) * tile
    blk_e = jnp.minimum(jnp.sum(pend[None, :] <= blk_first[:, None], axis=1, dtype=jnp.int32),
                        N_EXPERTS - 1)
    src = (blk_first - shift[blk_e])[:, None] + jnp.arange(tile, dtype=jnp.int32)[None, :]
    valid = src < (start + counts)[blk_e][:, None]
    row_tok = jnp.where(valid, order[jnp.clip(src, 0, n_assign - 1)] // TOP_K, 0).reshape(-1)
    nvalid = (pend[-1:] // tile).astype(jnp.int32)
    return row_tok, pos.reshape(m, TOP_K), blk_e, nvalid


def _final_kernel(x1_ref, yg_ref, gate_ref, p_ref, wpg_ref, bpg_ref, wpp_ref, g_ref, b_ref, *rest):
    o_ref = rest[-1]
    x1 = x1_ref[...]
    gate = gate_ref[...]
    ffn = jnp.zeros(x1.shape, F32)
    for j in range(TOP_K):
        ffn = ffn + gate[:, j:j + 1] * yg_ref[j].astype(F32)
    ple = jax.nn.sigmoid(_dot(x1, wpg_ref[...]) + bpg_ref[...]) * _dot(p_ref[...], wpp_ref[...])
    o_ref[...] = _layer_norm(DEEPNORM_ALPHA * x1 + ffn + ple, g_ref[...], b_ref[...], LN_EPS)


def _final(x1, yg, gate, p, wpg, bpg, wpp, g, b, layer, group, out_prev):
    tm = TOK_TILE
    mg = x1.shape[0]
    ft = group * (mg // tm)
    in_specs = [_row_spec(tm, D_MODEL),
                pl.BlockSpec((TOP_K, tm, D_MODEL), lambda i: (0, i, 0)),
                _row_spec(tm, LANES),
                pl.BlockSpec((None, tm, D_PLE), lambda i: (layer, i + ft, 0)),
                _const_spec((D_MODEL, D_MODEL), layer), _const_spec((1, D_MODEL), layer),
                _const_spec((D_PLE, D_MODEL), layer), _const_spec((1, D_MODEL), layer),
                _const_spec((1, D_MODEL), layer)]
    args = [x1, yg, gate, p, wpg, bpg, wpp, g, b]
    aliases = {}
    if out_prev is not None:
        in_specs.append(pl.BlockSpec(memory_space=pl.ANY))
        args.append(out_prev)
        aliases = {len(args) - 1: 0}
    return pl.pallas_call(
        _final_kernel, grid=(mg // tm,),
        in_specs=in_specs,
        out_specs=_row_spec(tm, D_MODEL, ft),
        out_shape=jax.ShapeDtypeStruct((mg * MOE_GROUPS, D_MODEL), F32),
        input_output_aliases=aliases,
        compiler_params=_cparams(), name="final")(*args)


def _pad_lowrank(w):
    z = jnp.zeros_like(w[:, 0])
    fwd = jnp.concatenate([w[:, 0], z], axis=1)
    bwd = jnp.concatenate([z, w[:, 1]], axis=1)
    return jnp.stack([fwd, bwd], axis=1)


def kernel(x_prompt, x_sample, p_prompt, p_sample, ln0_g, ln0_b, w_in, b_in, mu_shift, dw_w, dw_b, cln_g, cln_b, w_cb, b_cb, w0, w2, a0, a2, g2, k_k, k_a, r_k, lnx_g, lnx_b, w_rb, w_out, b_out, ln1_g, ln1_b, w_router, b_router, w_gu, b_gu, w_down, b_down, w_pg, b_pg, w_pp, ln2_g, ln2_b):
    bp, tp, d = x_prompt.shape
    bs, ts, _ = x_sample.shape
    mp, ms = bp * tp, bs * ts
    m = mp + ms
    assert d == D_MODEL and tp % TOK_TILE == 0 and ts % TOK_TILE == 0
    assert m % LN0_TILE == 0 and m % (MOE_GROUPS * TOK_TILE) == 0
    assert (m // MOE_GROUPS * TOP_K) % MOE_TILE == 0
    assert m % (WKV_INTRA_CHUNKS * WKV_CHUNK) == 0 and m % (WKV_INTER_CHUNKS * WKV_CHUNK) == 0
    seq = (mp, tp, ts, m)
    L = w_in.shape[0]

    x = jnp.concatenate([x_prompt.reshape(mp, d), x_sample.reshape(ms, d)], axis=0)
    p = jnp.concatenate([p_prompt.reshape(L, mp, D_PLE), p_sample.reshape(L, ms, D_PLE)], axis=1)

    row = lambda a: a.reshape(a.shape[0], 1, -1)
    wc = w_in[:, :, :N_GLU].astype(BF16)
    wz = w_in[:, :, N_GLU:N_GLU + N_SHIFT].astype(BF16)
    wg = w_in[:, :, N_GLU + N_SHIFT:].astype(BF16)
    bc, bz, bg = row(b_in[:, :N_GLU]), row(b_in[:, N_GLU:N_GLU + N_SHIFT]), row(b_in[:, N_GLU + N_SHIFT:])
    w2p = _pad_lowrank(w2).astype(BF16)
    a2p = _pad_lowrank(a2).astype(BF16)
    head = jnp.arange(D_RWKV, dtype=jnp.int32) // RWKV_HEAD
    hs = (head[:, None] == head[None, :]).astype(BF16)
    wr_pad = jnp.pad(w_router, ((0, 0), (0, 0), (0, LANES - N_EXPERTS)))
    wrh = wr_pad.astype(BF16)
    wrl = (wr_pad - wrh.astype(F32)).astype(BF16)
    br = row(jnp.pad(b_router, ((0, 0), (0, LANES - N_EXPERTS))))
    bgu = b_gu.reshape(L, N_EXPERTS, 1, 2 * D_EXPERT)
    bdn = b_down.reshape(L, N_EXPERTS, 1, D_MODEL)
    wcb_b, wrb_b, wout_b = w_cb.astype(BF16), w_rb.astype(BF16), w_out.astype(BF16)
    wpg_b, wpp_b, g2_b = w_pg.astype(BF16), w_pp.astype(BF16), g2.astype(BF16)

    x = _ln0(x, ln0_g, ln0_b)
    for l in range(L):
        c, z, gt = _in_proj(x, wc, wz, wg, bc, bz, bg, l)
        conv = _conv_branch(c, dw_w, row(dw_b), row(cln_g), row(cln_b), wcb_b, row(b_cb), l, seq)
        (r, v, kk, g, bonus, lwf, kdf, bf, lwb, kdb, bb) = _rwkv_prep(
            z, mu_shift, w0, w2p, a0, a2p, g2_b, row(k_k), row(k_a), row(r_k.reshape(L, D_RWKV)), hs, l, seq)
        yf, yb = _wkv_inter(*_wkv_intra(r, v, kk, lwf, kdf, bf, lwb, kdb, bb), seq)
        x_next = None
        after = jnp.zeros((MOE_TILE,), jnp.int32)
        for h in range(MOE_GROUPS):
            x1, x1b, tope, gate, rank, counts = _post(
                x, yf, yb, bonus, g, conv, gt, hs, row(lnx_g), row(lnx_b), wrb_b, wout_b, row(b_out),
                row(ln1_g), row(ln1_b), wrh, wrl, br, l, h, after)
            row_tok, pos, blk_e, nvalid = _route(
                tope[:, :TOP_K], rank[:, :TOP_K], counts[0, :N_EXPERTS], MOE_TILE)
            after = jnp.full((MOE_TILE,), jnp.max(row_tok))
            y_rows = _moe(blk_e, nvalid, x1b[row_tok], w_gu, bgu, w_down, bdn, l)
            x_next = _final(x1, y_rows[pos.T], gate, p, wpg_b, row(b_pg), wpp_b, row(ln2_g),
                            row(ln2_b), l, h, x_next)
        x = x_next

    return (x[:mp].reshape(bp, tp, d), x[mp:].reshape(bs, ts, d))
```
